```python
import math
import jax
import jax.numpy as jnp
from jax import lax
import numpy as np

D_MODEL = 2048
BATCH = 4
SEQ = 2048
DEPTH = 2

MLA_HEADS = 8
MLA_Q_RANK = 768
MLA_KV_RANK = 512
MLA_NOPE = 128
MLA_ROPE = 64
MLA_QK = MLA_NOPE + MLA_ROPE
MLA_V = 128
ROPE_THETA = 10000.0

CONV_CH = 1024
CONV_WIDTH = 31

MOBA_HEADS = 8
MOBA_HEAD_DIM = 128
MOBA_BLOCK = 256
MOBA_TOPK = 3
MOBA_Q_CHUNK = 32

ATTN_Q_BLOCK = 128

D_FF = 5632

N_BRANCH = 3
NORM_EPS = 1e-6
NEG_INF = -1e30

IN_SPLITS = (MLA_Q_RANK, MLA_KV_RANK, MLA_ROPE, 2 * CONV_CH, 3 * MOBA_HEADS * MOBA_HEAD_DIM, N_BRANCH * D_MODEL)
D_IN = MLA_Q_RANK + MLA_KV_RANK + MLA_ROPE + 2 * CONV_CH + 3 * MOBA_HEADS * MOBA_HEAD_DIM + N_BRANCH * D_MODEL

kernel_name = "hybrid_mla_conformer_moba_macaron"


def rms_norm(x, g):
    xf = x.astype(jnp.float32)
    y = xf * lax.rsqrt(jnp.mean(xf * xf, axis=-1, keepdims=True) + NORM_EPS)
    return (y * g.astype(jnp.float32)).astype(x.dtype)


def layer_norm(x, g, b):
    xf = x.astype(jnp.float32)
    xc = xf - jnp.mean(xf, axis=-1, keepdims=True)
    y = xc * lax.rsqrt(jnp.mean(xc * xc, axis=-1, keepdims=True) + NORM_EPS)
    return (y * g.astype(jnp.float32) + b.astype(jnp.float32)).astype(x.dtype)


def swiglu_ffn(x, w_gate, w_up, w_down):
    return (jax.nn.silu(x @ w_gate) * (x @ w_up)) @ w_down


def rope(x, pos):
    half = x.shape[-1] // 2
    inv_freq = jnp.exp(-math.log(ROPE_THETA) * jnp.arange(half, dtype=jnp.float32) * 2.0 / x.shape[-1])
    ang = pos.astype(jnp.float32)[:, None] * inv_freq[None, :]
    cos, sin = jnp.cos(ang), jnp.sin(ang)
    xf = x.astype(jnp.float32)
    x1, x2 = xf[..., :half], xf[..., half:]
    return jnp.concatenate([x1 * cos - x2 * sin, x1 * sin + x2 * cos], axis=-1).astype(x.dtype)


def alibi_slopes(n_heads):
    return jnp.exp2(-8.0 * jnp.arange(1, n_heads + 1, dtype=jnp.float32) / n_heads)


def causal_attention(q, k, v, scale):
    B, H, S, Dqk = q.shape
    nq = S // ATTN_Q_BLOCK
    qb = jnp.moveaxis(q.reshape(B, H, nq, ATTN_Q_BLOCK, Dqk), 2, 0)
    kpos = jnp.arange(S, dtype=jnp.int32)

    def one_block(args):
        qi, i = args
        qpos = i * ATTN_Q_BLOCK + jnp.arange(ATTN_Q_BLOCK, dtype=jnp.int32)
        s = jnp.einsum('bhqd,bhkd->bhqk', qi, k, preferred_element_type=jnp.float32) * scale
        s = jnp.where(kpos[None, :] <= qpos[:, None], s, NEG_INF)
        p = jax.nn.softmax(s, axis=-1)
        return jnp.einsum('bhqk,bhkd->bhqd', p.astype(v.dtype), v)

    out = lax.map(one_block, (qb, jnp.arange(nq, dtype=jnp.int32)))
    return jnp.moveaxis(out, 0, 2).reshape(B, H, S, v.shape[-1])


def moba_attention(q, k, v, slopes):
    B, H, S, Dh = q.shape
    L = MOBA_BLOCK
    nb = -(-S // L)
    pad = nb * L - S
    k_blk = jnp.pad(k, ((0, 0), (0, 0), (0, pad), (0, 0))).reshape(B, H, nb, L, Dh)
    v_blk = jnp.pad(v, ((0, 0), (0, 0), (0, pad), (0, 0))).reshape(B, H, nb, L, Dh)
    pos = jnp.arange(S, dtype=jnp.int32)
    own = pos // L
    own_idx = jnp.broadcast_to(own[None, None, :, None], (B, H, S, 1))
    n_sel = min(MOBA_TOPK, nb - 1)
    if n_sel > 0:
        k_mean = jnp.mean(k_blk.astype(jnp.float32), axis=3)
        gate = jnp.einsum('bhsd,bhnd->bhsn', q.astype(jnp.float32), k_mean)
        fully_past = jnp.arange(nb, dtype=jnp.int32)[None, :] < own[:, None]
        gate = jnp.where(fully_past, gate, NEG_INF)
        _, top_idx = lax.top_k(gate, n_sel)
        top_idx = top_idx.astype(jnp.int32)
        idx = jnp.concatenate([top_idx, own_idx], axis=-1)
        valid = jnp.concatenate([top_idx < own[:, None], jnp.ones_like(own_idx, dtype=bool)], axis=-1)
    else:
        idx = own_idx
        valid = jnp.ones_like(own_idx, dtype=bool)
    J = idx.shape[-1]
    C = MOBA_Q_CHUNK
    nc = S // C
    scale = Dh ** -0.5
    bi = jnp.arange(B)[:, None, None, None]
    hi = jnp.arange(H)[None, :, None, None]

    def to_chunks(a):
        return jnp.moveaxis(a.reshape(B, H, nc, C, a.shape[-1]), 2, 0)

    def one_chunk(args):
        qc, idc, vc, qpos = args
        k_sel = k_blk[bi, hi, idc]
        v_sel = v_blk[bi, hi, idc]
        kpos = idc[..., None] * L + jnp.arange(L, dtype=jnp.int32)
        dist = qpos[None, None, :, None, None] - kpos
        s = jnp.einsum('bhcd,bhcjld->bhcjl', qc, k_sel, preferred_element_type=jnp.float32) * scale
        s = s - slopes[None, :, None, None, None] * dist.astype(jnp.float32)
        s = jnp.where(vc[..., None] & (dist >= 0), s, NEG_INF)
        p = jax.nn.softmax(s.reshape(B, H, C, J * L), axis=-1).reshape(B, H, C, J, L)
        return jnp.einsum('bhcjl,bhcjld->bhcd', p.astype(v.dtype), v_sel)

    out = lax.map(one_chunk, (to_chunks(q), to_chunks(idx), to_chunks(valid), pos.reshape(nc, C)))
    return jnp.moveaxis(out, 0, 2).reshape(B, H, S, Dh)


def mla_branch(c_q, c_kv, k_r, g_cq, g_ckv, w_uq, w_ukv, g_qn, g_kn, w_o):
    B, S, _ = c_q.shape
    pos = jnp.arange(S, dtype=jnp.int32)
    q = (rms_norm(c_q, g_cq) @ w_uq).reshape(B, S, MLA_HEADS, MLA_QK).transpose(0, 2, 1, 3)
    kv = (rms_norm(c_kv, g_ckv) @ w_ukv).reshape(B, S, MLA_HEADS, MLA_NOPE + MLA_V).transpose(0, 2, 1, 3)
    k_nope, v = kv[..., :MLA_NOPE], kv[..., MLA_NOPE:]
    k_rot = jnp.broadcast_to(k_r[:, None], (B, MLA_HEADS, S, MLA_ROPE))
    k = jnp.concatenate([k_nope, k_rot], axis=-1)
    q = rms_norm(q, g_qn)
    k = rms_norm(k, g_kn)
    q = jnp.concatenate([q[..., :MLA_NOPE], rope(q[..., MLA_NOPE:], pos)], axis=-1)
    k = jnp.concatenate([k[..., :MLA_NOPE], rope(k[..., MLA_NOPE:], pos)], axis=-1)
    o = causal_attention(q, k, v, MLA_QK ** -0.5)
    return o.transpose(0, 2, 1, 3).reshape(B, S, MLA_HEADS * MLA_V) @ w_o


def conv_branch(u, w_dw, b_dw, g_ln, b_ln, w_pw):
    a, gate = jnp.split(u, 2, axis=-1)
    z = a * jax.nn.sigmoid(gate)
    z = lax.conv_general_dilated(z, w_dw[:, None, :], window_strides=(1,),
                                 padding=[(CONV_WIDTH - 1, 0)],
                                 dimension_numbers=('NWC', 'WIO', 'NWC'),
                                 feature_group_count=CONV_CH) + b_dw
    z = jax.nn.silu(layer_norm(z, g_ln, b_ln))
    return z @ w_pw


def moba_branch(qkv, g_qn, g_kn, w_o):
    B, S, _ = qkv.shape
    q, k, v = [t.reshape(B, S, MOBA_HEADS, MOBA_HEAD_DIM).transpose(0, 2, 1, 3) for t in jnp.split(qkv, 3, axis=-1)]
    q = rms_norm(q, g_qn)
    k = rms_norm(k, g_kn)
    o = moba_attention(q, k, v, alibi_slopes(MOBA_HEADS))
    return o.transpose(0, 2, 1, 3).reshape(B, S, MOBA_HEADS * MOBA_HEAD_DIM) @ w_o


def mixer_layer(h, w_in, b_gate, mla_cq_norm, mla_ckv_norm, mla_w_uq, mla_w_ukv, mla_q_norm, mla_k_norm, mla_w_o,
                conv_w_dw, conv_b_dw, conv_ln_g, conv_ln_b, conv_w_pw, moba_q_norm, moba_k_norm, moba_w_o, w_out):
    B, S, D = h.shape
    u = h @ w_in
    split_at = [int(i) for i in np.cumsum(IN_SPLITS)[:-1]]
    c_q, c_kv, k_r, conv_in, moba_qkv, gate_logits = jnp.split(u, split_at, axis=-1)
    y_mla = mla_branch(c_q, c_kv, k_r, mla_cq_norm, mla_ckv_norm, mla_w_uq, mla_w_ukv, mla_q_norm, mla_k_norm, mla_w_o)
    y_conv = conv_branch(conv_in, conv_w_dw, conv_b_dw, conv_ln_g, conv_ln_b, conv_w_pw)
    y_moba = moba_branch(moba_qkv, moba_q_norm, moba_k_norm, moba_w_o)
    g = jax.nn.sigmoid(gate_logits + b_gate).reshape(B, S, N_BRANCH, D)
    merged = g[:, :, 0] * y_mla + g[:, :, 1] * y_conv + g[:, :, 2] * y_moba
    return merged @ w_out


def setup_inputs(seed: int = 0) -> dict:
    key = jax.random.key(seed)
    ks = iter(jax.random.split(key, 40))

    def nrm(shape, scale):
        return scale * jax.random.normal(next(ks), shape, jnp.float32)

    def gain(shape):
        return 1.0 + 0.02 * jax.random.normal(next(ks), shape, jnp.float32)

    L, D = DEPTH, D_MODEL
    return {
        'x': nrm((BATCH, SEQ, D), 1.0),
        'ffn1_norm': gain((L, D)),
        'ffn1_w_gate': nrm((L, D, D_FF), D ** -0.5),
        'ffn1_w_up': nrm((L, D, D_FF), D ** -0.5),
        'ffn1_w_down': nrm((L, D_FF, D), D_FF ** -0.5),
        'mix_norm': gain((L, D)),
        'w_in': nrm((L, D, D_IN), D ** -0.5),
        'b_gate': nrm((L, N_BRANCH * D), 0.02),
        'mla_cq_norm': gain((L, MLA_Q_RANK)),
        'mla_ckv_norm': gain((L, MLA_KV_RANK)),
        'mla_w_uq': nrm((L, MLA_Q_RANK, MLA_HEADS * MLA_QK), MLA_Q_RANK ** -0.5),
        'mla_w_ukv': nrm((L, MLA_KV_RANK, MLA_HEADS * (MLA_NOPE + MLA_V)), MLA_KV_RANK ** -0.5),
        'mla_q_norm': gain((L, MLA_QK)),
        'mla_k_norm': gain((L, MLA_QK)),
        'mla_w_o': nrm((L, MLA_HEADS * MLA_V, D), (MLA_HEADS * MLA_V) ** -0.5),
        'conv_w_dw': nrm((L, CONV_WIDTH, CONV_CH), CONV_WIDTH ** -0.5),
        'conv_b_dw': nrm((L, CONV_CH), 0.02),
        'conv_ln_g': gain((L, CONV_CH)),
        'conv_ln_b': nrm((L, CONV_CH), 0.02),
        'conv_w_pw': nrm((L, CONV_CH, D), CONV_CH ** -0.5),
        'moba_q_norm': gain((L, MOBA_HEAD_DIM)),
        'moba_k_norm': gain((L, MOBA_HEAD_DIM)),
        'moba_w_o': nrm((L, MOBA_HEADS * MOBA_HEAD_DIM, D), (MOBA_HEADS * MOBA_HEAD_DIM) ** -0.5),
        'w_out': nrm((L, D, D), D ** -0.5),
        'ffn2_norm': gain((L, D)),
        'ffn2_w_gate': nrm((L, D, D_FF), D ** -0.5),
        'ffn2_w_up': nrm((L, D, D_FF), D ** -0.5),
        'ffn2_w_down': nrm((L, D_FF, D), D_FF ** -0.5),
    }


def reference(x, ffn1_norm, ffn1_w_gate, ffn1_w_up, ffn1_w_down, mix_norm, w_in, b_gate,
              mla_cq_norm, mla_ckv_norm, mla_w_uq, mla_w_ukv, mla_q_norm, mla_k_norm, mla_w_o,
              conv_w_dw, conv_b_dw, conv_ln_g, conv_ln_b, conv_w_pw,
              moba_q_norm, moba_k_norm, moba_w_o, w_out,
              ffn2_norm, ffn2_w_gate, ffn2_w_up, ffn2_w_down):
    for l in range(DEPTH):
        x = x + 0.5 * swiglu_ffn(rms_norm(x, ffn1_norm[l]), ffn1_w_gate[l], ffn1_w_up[l], ffn1_w_down[l])
        x = x + mixer_layer(rms_norm(x, mix_norm[l]), w_in[l], b_gate[l],
                            mla_cq_norm[l], mla_ckv_norm[l], mla_w_uq[l], mla_w_ukv[l],
                            mla_q_norm[l], mla_k_norm[l], mla_w_o[l],
                            conv_w_dw[l], conv_b_dw[l], conv_ln_g[l], conv_ln_b[l], conv_w_pw[l],
                            moba_q_norm[l], moba_k_norm[l], moba_w_o[l], w_out[l])
        x = x + 0.5 * swiglu_ffn(rms_norm(x, ffn2_norm[l]), ffn2_w_gate[l], ffn2_w_up[l], ffn2_w_down[l])
    return x
```

```python
import functools
import math

import jax
import jax.numpy as jnp
from jax import lax
from jax.experimental import pallas as pl
from jax.experimental.pallas import tpu as pltpu

D_MODEL = 2048
BATCH = 4
SEQ = 2048
DEPTH = 2
TOKENS = BATCH * SEQ

MLA_HEADS = 8
MLA_Q_RANK = 768
MLA_KV_RANK = 512
MLA_NOPE = 128
MLA_ROPE = 64
MLA_QK = MLA_NOPE + MLA_ROPE
MLA_V = 128
ROPE_THETA = 10000.0

CONV_CH = 1024
CONV_WIDTH = 31
CONV_HALO = 32

MOBA_HEADS = 8
MOBA_HEAD_DIM = 128
MOBA_BLOCK = 256
MOBA_TOPK = 3
MOBA_NB = SEQ // MOBA_BLOCK

D_FF = 5632
N_BRANCH = 3
NORM_EPS = 1e-6
NEG_INF = -1e30

OFF_CQ = 0
OFF_CKV = OFF_CQ + MLA_Q_RANK
OFF_KR = OFF_CKV + MLA_KV_RANK
OFF_CONV = OFF_KR + MLA_ROPE
OFF_MOBA = OFF_CONV + 2 * CONV_CH
OFF_GATE = OFF_MOBA + 3 * MOBA_HEADS * MOBA_HEAD_DIM

F32 = jnp.float32
BF16 = jnp.bfloat16

VMEM_LIMIT_BYTES = 56 * 1024 * 1024

TM_FFN = 512
TF_FFN = 512
TM_PROJ = 512
TN_CONV = 512
TM_MERGE = 512
TN_MERGE = 256
TQ_MLA = 256
TS_CONV = 256
CONV_ROWS = 32


def _params(*semantics):
    return pltpu.CompilerParams(dimension_semantics=semantics, vmem_limit_bytes=VMEM_LIMIT_BYTES)


def _rms_bf16(x, g):
    ms = jnp.mean(x * x, axis=-1, keepdims=True)
    return (x * lax.rsqrt(ms + NORM_EPS) * g).astype(BF16)


def _dot(a, b):
    return jnp.dot(a, b, preferred_element_type=F32)


def _dot_nt(a, b):
    return lax.dot_general(a, b, (((1,), (1,)), ((), ())), preferred_element_type=F32)


def _ffn_body(x_ref, g_ref, wg_ref, wu_ref, wd_ref, o_ref, h_scr, acc_scr):
    j = pl.program_id(1)

    @pl.when(j == 0)
    def _():
        h_scr[...] = _rms_bf16(x_ref[...], g_ref[...])

    h = h_scr[...]
    a = _dot(h, wg_ref[...])
    b = _dot(h, wu_ref[...])
    t = (a * jax.nn.sigmoid(a) * b).astype(BF16)
    p = _dot(t, wd_ref[...])

    @pl.when(j == 0)
    def _():
        acc_scr[...] = p

    @pl.when(j > 0)
    def _():
        acc_scr[...] += p

    @pl.when(j == pl.num_programs(1) - 1)
    def _():
        o_ref[...] = x_ref[...] + 0.5 * acc_scr[...]


def _ffn(x, g, wg, wu, wd, l):
    tm, tf = TM_FFN, TF_FFN
    return pl.pallas_call(
        _ffn_body,
        grid=(TOKENS // tm, D_FF // tf),
        in_specs=[
            pl.BlockSpec((tm, D_MODEL), lambda i, j: (i, 0)),
            pl.BlockSpec((1, D_MODEL), lambda i, j: (0, 0)),
            pl.BlockSpec((None, D_MODEL, tf), lambda i, j: (l, 0, j)),
            pl.BlockSpec((None, D_MODEL, tf), lambda i, j: (l, 0, j)),
            pl.BlockSpec((None, tf, D_MODEL), lambda i, j: (l, j, 0)),
        ],
        out_specs=pl.BlockSpec((tm, D_MODEL), lambda i, j: (i, 0)),
        out_shape=jax.ShapeDtypeStruct((TOKENS, D_MODEL), F32),
        scratch_shapes=[pltpu.VMEM((tm, D_MODEL), BF16), pltpu.VMEM((tm, D_MODEL), F32)],
        compiler_params=_params("parallel", "arbitrary"),
        name="ffn",
    )(x, g, wg, wu, wd)


def _mla_proj_body(x_ref, gmix_ref, wc_ref, wkr_ref, wkrr_ref, gcq_ref, gckv_ref,
                   wqn_ref, wqr_ref, wqrr_ref, wkn_ref, wv_ref,
                   gqn_ref, gqr_ref, gqrr_ref, gkn_ref, gkr_ref, gkrr_ref,
                   cos_ref, sin_ref, q_ref, k_ref, v_ref):
    hn = _rms_bf16(x_ref[...], gmix_ref[...])
    c = _dot(hn, wc_ref[...])
    cq = c[:, :MLA_Q_RANK]
    ckv = c[:, MLA_Q_RANK:]
    kr = _dot(hn, wkr_ref[...])
    krr = _dot(hn, wkrr_ref[...])
    cqn = _rms_bf16(cq, gcq_ref[...])
    ckvn = _rms_bf16(ckv, gckv_ref[...])
    cos = cos_ref[...]
    sin = sin_ref[...]
    kr_ss = jnp.sum(kr * kr, axis=-1, keepdims=True)
    kr_rot = kr * gkr_ref[...] * cos + krr * gkrr_ref[...] * sin
    q_scale = MLA_QK ** -0.5
    for h in range(MLA_HEADS):
        qn = _dot(cqn, wqn_ref[h])
        qr = _dot(cqn, wqr_ref[h])
        qrr = _dot(cqn, wqrr_ref[h])
        ss = (jnp.sum(qn * qn, axis=-1, keepdims=True) + jnp.sum(qr * qr, axis=-1, keepdims=True)) / MLA_QK
        r = lax.rsqrt(ss + NORM_EPS) * q_scale
        q_ref[0, h, :, :MLA_NOPE] = (qn * r * gqn_ref[...]).astype(BF16)
        q_ref[0, h, :, MLA_NOPE:] = ((qr * gqr_ref[...] * cos + qrr * gqrr_ref[...] * sin) * r).astype(BF16)
        kn = _dot(ckvn, wkn_ref[h])
        ss = (jnp.sum(kn * kn, axis=-1, keepdims=True) + kr_ss) / MLA_QK
        r = lax.rsqrt(ss + NORM_EPS)
        k_ref[0, h, :, :MLA_NOPE] = (kn * r * gkn_ref[...]).astype(BF16)
        k_ref[0, h, :, MLA_NOPE:] = (kr_rot * r).astype(BF16)
        v_ref[0, h] = _dot(ckvn, wv_ref[h]).astype(BF16)


def _rot_cols(w):
    half = w.shape[-1] // 2
    return jnp.concatenate([-w[..., half:], w[..., :half]], axis=-1)


def _swap_halves(g):
    half = g.shape[-1] // 2
    return jnp.concatenate([g[..., half:], g[..., :half]], axis=-1)


def _mla_proj(x, gmix, w_in_l, gcq, gckv, w_uq, w_ukv, gq, gk, cos, sin):
    tm = TM_PROJ
    n_s = SEQ // tm
    wc = w_in_l[:, OFF_CQ:OFF_KR].astype(BF16)
    wkr_f = w_in_l[:, OFF_KR:OFF_CONV]
    wkr = wkr_f.astype(BF16)
    wkrr = _rot_cols(wkr_f).astype(BF16)
    wq = w_uq.reshape(MLA_Q_RANK, MLA_HEADS, MLA_QK).transpose(1, 0, 2)
    wqn = wq[..., :MLA_NOPE].astype(BF16)
    wqr = wq[..., MLA_NOPE:].astype(BF16)
    wqrr = _rot_cols(wq[..., MLA_NOPE:]).astype(BF16)
    wkv = w_ukv.reshape(MLA_KV_RANK, MLA_HEADS, MLA_NOPE + MLA_V).transpose(1, 0, 2)
    wkn = wkv[..., :MLA_NOPE].astype(BF16)
    wv = wkv[..., MLA_NOPE:].astype(BF16)
    gq = gq.reshape(1, MLA_QK)
    gk = gk.reshape(1, MLA_QK)
    gqn, gqr, gqrr = gq[:, :MLA_NOPE], gq[:, MLA_NOPE:], _swap_halves(gq[:, MLA_NOPE:])
    gkn, gkr, gkrr = gk[:, :MLA_NOPE], gk[:, MLA_NOPE:], _swap_halves(gk[:, MLA_NOPE:])

    def full(a):
        nd = a.ndim
        return pl.BlockSpec(a.shape, lambda i: (0,) * nd)

    ins = [x, gmix, wc, wkr, wkrr, gcq.reshape(1, -1), gckv.reshape(1, -1), wqn, wqr, wqrr, wkn, wv,
           gqn, gqr, gqrr, gkn, gkr, gkrr, cos, sin]
    in_specs = [pl.BlockSpec((tm, D_MODEL), lambda i: (i, 0))] + [full(a) for a in ins[1:-2]]
    in_specs += [pl.BlockSpec((tm, MLA_ROPE), lambda i: (i % n_s, 0))] * 2
    head_spec = lambda d: pl.BlockSpec((1, MLA_HEADS, tm, d), lambda i: (i // n_s, 0, i % n_s, 0))
    return pl.pallas_call(
        _mla_proj_body,
        grid=(TOKENS // tm,),
        in_specs=in_specs,
        out_specs=[head_spec(MLA_QK), head_spec(MLA_QK), head_spec(MLA_V)],
        out_shape=[jax.ShapeDtypeStruct((BATCH, MLA_HEADS, SEQ, MLA_QK), BF16),
                   jax.ShapeDtypeStruct((BATCH, MLA_HEADS, SEQ, MLA_QK), BF16),
                   jax.ShapeDtypeStruct((BATCH, MLA_HEADS, SEQ, MLA_V), BF16)],
        compiler_params=_params("parallel"),
        name="mla_proj",
    )(*ins)


def _softmax_step(carry, s, v):
    m, l, acc = carry
    m_new = jnp.maximum(m, jnp.max(s, axis=-1, keepdims=True))
    p = jnp.exp(s - m_new)
    alpha = jnp.exp(m - m_new)
    l = alpha * l + jnp.sum(p, axis=-1, keepdims=True)
    acc = alpha * acc + _dot(p.astype(BF16), v)
    return m_new, l, acc


def _softmax_first(s, v):
    m = jnp.max(s, axis=-1, keepdims=True)
    p = jnp.exp(s - m)
    return m, jnp.sum(p, axis=-1, keepdims=True), _dot(p.astype(BF16), v)


def _mla_attn_body(q_ref, k_ref, v_ref, o_ref):
    t = TQ_MLA
    qi = pl.program_id(2)
    q = q_ref[0, 0]
    row = lax.broadcasted_iota(jnp.int32, (t, t), 0)
    col = lax.broadcasted_iota(jnp.int32, (t, t), 1)
    d0 = pl.multiple_of(qi * t, t)
    s = _dot_nt(q, k_ref[0, 0, pl.ds(d0, t), :])
    s = jnp.where(col <= row, s, NEG_INF)
    carry = _softmax_first(s, v_ref[0, 0, pl.ds(d0, t), :])

    def body(j, carry):
        j0 = pl.multiple_of(j * t, t)
        s = _dot_nt(q, k_ref[0, 0, pl.ds(j0, t), :])
        return _softmax_step(carry, s, v_ref[0, 0, pl.ds(j0, t), :])

    m, l, acc = lax.fori_loop(0, qi, body, carry)
    o_ref[0] = (acc / l).astype(BF16)


def _mla_attn(q, k, v):
    t = TQ_MLA
    return pl.pallas_call(
        _mla_attn_body,
        grid=(BATCH, MLA_HEADS, SEQ // t),
        in_specs=[
            pl.BlockSpec((1, 1, t, MLA_QK), lambda b, h, i: (b, h, i, 0)),
            pl.BlockSpec((1, 1, SEQ, MLA_QK), lambda b, h, i: (b, h, 0, 0)),
            pl.BlockSpec((1, 1, SEQ, MLA_V), lambda b, h, i: (b, h, 0, 0)),
        ],
        out_specs=pl.BlockSpec((1, t, MLA_V), lambda b, h, i: (b, i, h)),
        out_shape=jax.ShapeDtypeStruct((BATCH, SEQ, MLA_HEADS * MLA_V), BF16),
        compiler_params=_params("parallel", "parallel", "arbitrary"),
        name="mla_attn",
    )(q, k, v)


def _glu_proj_body(x_ref, gmix_ref, wa_ref, wg_ref, z_ref, h_scr):
    @pl.when(pl.program_id(1) == 0)
    def _():
        h_scr[...] = _rms_bf16(x_ref[...], gmix_ref[...])

    h = h_scr[...]
    a = _dot(h, wa_ref[...])
    g = _dot(h, wg_ref[...])
    z_ref[...] = a * jax.nn.sigmoid(g)


def _glu_proj(x, gmix, w_in_l):
    tm, tn = TM_PROJ, TN_CONV
    wa = w_in_l[:, OFF_CONV:OFF_CONV + CONV_CH].astype(BF16)
    wg = w_in_l[:, OFF_CONV + CONV_CH:OFF_MOBA].astype(BF16)
    return pl.pallas_call(
        _glu_proj_body,
        grid=(TOKENS // tm, CONV_CH // tn),
        in_specs=[
            pl.BlockSpec((tm, D_MODEL), lambda i, j: (i, 0)),
            pl.BlockSpec((1, D_MODEL), lambda i, j: (0, 0)),
            pl.BlockSpec((D_MODEL, tn), lambda i, j: (0, j)),
            pl.BlockSpec((D_MODEL, tn), lambda i, j: (0, j)),
        ],
        out_specs=pl.BlockSpec((tm, tn), lambda i, j: (i, j)),
        out_shape=jax.ShapeDtypeStruct((TOKENS, CONV_CH), F32),
        scratch_shapes=[pltpu.VMEM((tm, D_MODEL), BF16)],
        compiler_params=_params("parallel", "arbitrary"),
        name="glu_proj",
    )(x, gmix, wa, wg)


def _conv_body(z_ref, w_ref, b_ref, lg_ref, lb_ref, o_ref, zbuf):
    ts, halo = TS_CONV, CONV_HALO
    si = pl.program_id(1)

    @pl.when(si == 0)
    def _():
        zbuf[0:halo, :] = jnp.zeros((halo, CONV_CH), F32)

    @pl.when(si > 0)
    def _():
        zbuf[0:halo, :] = zbuf[ts:ts + halo, :]

    zbuf[halo:halo + ts, :] = z_ref[...]
    shift = halo - (CONV_WIDTH - 1)
    for r0 in range(0, ts, CONV_ROWS):
        acc = jnp.zeros((CONV_ROWS, CONV_CH), F32) + b_ref[...]
        for k in range(CONV_WIDTH):
            acc = acc + zbuf[r0 + shift + k:r0 + shift + k + CONV_ROWS, :] * w_ref[k:k + 1, :]
        mu = jnp.mean(acc, axis=-1, keepdims=True)
        xc = acc - mu
        var = jnp.mean(xc * xc, axis=-1, keepdims=True)
        y = xc * lax.rsqrt(var + NORM_EPS) * lg_ref[...] + lb_ref[...]
        o_ref[r0:r0 + CONV_ROWS, :] = (y * jax.nn.sigmoid(y)).astype(BF16)


def _conv(z, w_dw, b_dw, ln_g, ln_b):
    ts = TS_CONV
    n_s = SEQ // ts
    vec = pl.BlockSpec((1, CONV_CH), lambda b, s: (0, 0))
    return pl.pallas_call(
        _conv_body,
        grid=(BATCH, n_s),
        in_specs=[
            pl.BlockSpec((ts, CONV_CH), lambda b, s: (b * n_s + s, 0)),
            pl.BlockSpec((CONV_WIDTH, CONV_CH), lambda b, s: (0, 0)),
            vec, vec, vec,
        ],
        out_specs=pl.BlockSpec((ts, CONV_CH), lambda b, s: (b * n_s + s, 0)),
        out_shape=jax.ShapeDtypeStruct((TOKENS, CONV_CH), BF16),
        scratch_shapes=[pltpu.VMEM((ts + CONV_HALO, CONV_CH), F32)],
        compiler_params=_params("arbitrary", "arbitrary"),
        name="conv",
    )(z, w_dw, b_dw.reshape(1, -1), ln_g.reshape(1, -1), ln_b.reshape(1, -1))


def _moba_proj_body(x_ref, gmix_ref, w_ref, g_ref, o_ref, h_scr):
    j = pl.program_id(1)

    @pl.when(j == 0)
    def _():
        h_scr[...] = _rms_bf16(x_ref[...], gmix_ref[...])

    u = _dot(h_scr[...], w_ref[...])
    g = g_ref[0]
    for h in range(MOBA_HEADS):
        uh = u[:, h * MOBA_HEAD_DIM:(h + 1) * MOBA_HEAD_DIM]
        ms = jnp.mean(uh * uh, axis=-1, keepdims=True)
        normed = uh * lax.rsqrt(ms + NORM_EPS) * g
        o_ref[0, 0, h] = jnp.where(j < 2, normed, uh)


def _moba_proj(x, gmix, w_in_l, gq, gk):
    tm = TM_PROJ
    n_s = SEQ // tm
    hd = MOBA_HEADS * MOBA_HEAD_DIM
    w = w_in_l[:, OFF_MOBA:OFF_GATE].astype(BF16)
    g = jnp.stack([gq, gk, jnp.ones_like(gq)]).reshape(3, 1, MOBA_HEAD_DIM)
    return pl.pallas_call(
        _moba_proj_body,
        grid=(TOKENS // tm, 3),
        in_specs=[
            pl.BlockSpec((tm, D_MODEL), lambda i, j: (i, 0)),
            pl.BlockSpec((1, D_MODEL), lambda i, j: (0, 0)),
            pl.BlockSpec((D_MODEL, hd), lambda i, j: (0, j)),
            pl.BlockSpec((1, 1, MOBA_HEAD_DIM), lambda i, j: (j, 0, 0)),
        ],
        out_specs=pl.BlockSpec((1, 1, MOBA_HEADS, tm, MOBA_HEAD_DIM),
                               lambda i, j: (j, i // n_s, 0, i % n_s, 0)),
        out_shape=jax.ShapeDtypeStruct((3, BATCH, MOBA_HEADS, SEQ, MOBA_HEAD_DIM), F32),
        scratch_shapes=[pltpu.VMEM((tm, D_MODEL), BF16)],
        compiler_params=_params("parallel", "arbitrary"),
        name="moba_proj",
    )(x, gmix, w, g)


def _moba_attn_body(slope_ref, q_ref, k_ref, v_ref, o_ref, kmean_scr):
    blk, nb = MOBA_BLOCK, MOBA_NB
    n = pl.program_id(2)

    @pl.when(n == 0)
    def _():
        for j in range(nb):
            kmean_scr[j:j + 1, :] = jnp.mean(k_ref[0, 0, 0, j * blk:(j + 1) * blk, :], axis=0, keepdims=True)

    q = q_ref[0, 0, 0]
    km = kmean_scr[...]
    lane = lax.broadcasted_iota(jnp.int32, (blk, nb), 1)
    cols = []
    gate = jnp.zeros((blk, nb), F32)
    for j in range(nb):
        gj = jnp.sum(q * km[j:j + 1, :], axis=-1, keepdims=True)
        gj = jnp.where(j < n, gj, NEG_INF)
        cols.append(gj)
        gate = jnp.where(lane == j, gj, gate)
    rank = jnp.zeros((blk, nb), jnp.int32)
    for j in range(nb):
        beats = (cols[j] > gate) | ((cols[j] == gate) & (lane > j))
        rank = rank + jnp.where(beats, 1, 0)
    selected = (rank < MOBA_TOPK) & (lane < n)
    row_bias = jnp.where(selected, 0.0, NEG_INF)

    slope = slope_ref[0][:, 0:1]
    qb = (q * (MOBA_HEAD_DIM ** -0.5)).astype(BF16)
    row = lax.broadcasted_iota(jnp.int32, (blk, blk), 0)
    col = lax.broadcasted_iota(jnp.int32, (blk, blk), 1)
    rel = (row - col).astype(F32)

    d0 = pl.multiple_of(n * blk, blk)
    s = _dot_nt(qb, k_ref[0, 0, 0, pl.ds(d0, blk), :].astype(BF16)) - slope * rel
    s = jnp.where(col <= row, s, NEG_INF)
    carry = _softmax_first(s, v_ref[0, 0, 0, pl.ds(d0, blk), :].astype(BF16))

    def body(j, carry):
        j0 = pl.multiple_of(j * blk, blk)
        bias = jnp.sum(jnp.where(lane == j, row_bias, 0.0), axis=-1, keepdims=True)
        dist0 = jnp.full((1, 1), (n - j) * blk, jnp.int32).astype(F32)
        s = _dot_nt(qb, k_ref[0, 0, 0, pl.ds(j0, blk), :].astype(BF16)) - slope * (rel + dist0) + bias
        return _softmax_step(carry, s, v_ref[0, 0, 0, pl.ds(j0, blk), :].astype(BF16))

    m, l, acc = lax.fori_loop(0, n, body, carry)
    o_ref[0] = (acc / l).astype(BF16)


def _moba_attn(qkv, slopes):
    blk = MOBA_BLOCK
    kv_spec = lambda which: pl.BlockSpec((1, 1, 1, SEQ, MOBA_HEAD_DIM), lambda b, h, n: (which, b, h, 0, 0))
    return pl.pallas_call(
        _moba_attn_body,
        grid=(BATCH, MOBA_HEADS, MOBA_NB),
        in_specs=[
            pl.BlockSpec((1, 1, 128), lambda b, h, n: (h, 0, 0)),
            pl.BlockSpec((1, 1, 1, blk, MOBA_HEAD_DIM), lambda b, h, n: (0, b, h, n, 0)),
            kv_spec(1),
            kv_spec(2),
        ],
        out_specs=pl.BlockSpec((1, blk, MOBA_HEAD_DIM), lambda b, h, n: (b, n, h)),
        out_shape=jax.ShapeDtypeStruct((BATCH, SEQ, MOBA_HEADS * MOBA_HEAD_DIM), BF16),
        scratch_shapes=[pltpu.VMEM((MOBA_NB, MOBA_HEAD_DIM), F32)],
        compiler_params=_params("parallel", "parallel", "arbitrary"),
        name="moba_attn",
    )(slopes, qkv, qkv, qkv)


def _merge_body(x_ref, gmix_ref, oa_ref, ob_ref, oc_ref, wg_ref, bg_ref, woa_ref, wob_ref, woc_ref,
                wout_ref, o_ref, h_scr, acc_scr):
    j = pl.program_id(1)

    @pl.when(j == 0)
    def _():
        h_scr[...] = _rms_bf16(x_ref[...], gmix_ref[...])

    h = h_scr[...]
    merged = None
    for i, (o_i, w_i) in enumerate(((oa_ref, woa_ref), (ob_ref, wob_ref), (oc_ref, woc_ref))):
        gate = jax.nn.sigmoid(_dot(h, wg_ref[i]) + bg_ref[i])
        term = gate * _dot(o_i[...], w_i[...])
        merged = term if merged is None else merged + term
    p = _dot(merged.astype(BF16), wout_ref[...])

    @pl.when(j == 0)
    def _():
        acc_scr[...] = p

    @pl.when(j > 0)
    def _():
        acc_scr[...] += p

    @pl.when(j == pl.num_programs(1) - 1)
    def _():
        o_ref[...] = x_ref[...] + acc_scr[...]


def _merge(x, gmix, oa, ob, oc, w_in_l, b_gate, w_oa, w_ob, w_oc, w_out):
    tm, tn = TM_MERGE, TN_MERGE
    wg = w_in_l[:, OFF_GATE:].astype(BF16).reshape(D_MODEL, N_BRANCH, D_MODEL).transpose(1, 0, 2)
    bg = b_gate.reshape(N_BRANCH, 1, D_MODEL)
    tok = lambda d: pl.BlockSpec((tm, d), lambda i, j: (i, 0))
    wo = lambda d: pl.BlockSpec((d, tn), lambda i, j: (0, j))
    return pl.pallas_call(
        _merge_body,
        grid=(TOKENS // tm, D_MODEL // tn),
        in_specs=[
            tok(D_MODEL),
            pl.BlockSpec((1, D_MODEL), lambda i, j: (0, 0)),
            tok(MLA_HEADS * MLA_V), tok(CONV_CH), tok(MOBA_HEADS * MOBA_HEAD_DIM),
            pl.BlockSpec((N_BRANCH, D_MODEL, tn), lambda i, j: (0, 0, j)),
            pl.BlockSpec((N_BRANCH, 1, tn), lambda i, j: (0, 0, j)),
            wo(MLA_HEADS * MLA_V), wo(CONV_CH), wo(MOBA_HEADS * MOBA_HEAD_DIM),
            pl.BlockSpec((tn, D_MODEL), lambda i, j: (j, 0)),
        ],
        out_specs=tok(D_MODEL),
        out_shape=jax.ShapeDtypeStruct((TOKENS, D_MODEL), F32),
        scratch_shapes=[pltpu.VMEM((tm, D_MODEL), BF16), pltpu.VMEM((tm, D_MODEL), F32)],
        compiler_params=_params("parallel", "arbitrary"),
        name="merge",
    )(x, gmix, oa, ob, oc, wg, bg, w_oa.astype(BF16), w_ob.astype(BF16), w_oc.astype(BF16),
      w_out.astype(BF16))


def _rope_tables():
    half = MLA_ROPE // 2
    inv_freq = jnp.exp(-math.log(ROPE_THETA) * jnp.arange(half, dtype=F32) * 2.0 / MLA_ROPE)
    ang = jnp.arange(SEQ, dtype=jnp.int32).astype(F32)[:, None] * inv_freq[None, :]
    cos, sin = jnp.cos(ang), jnp.sin(ang)
    return jnp.concatenate([cos, cos], axis=-1), jnp.concatenate([sin, sin], axis=-1)


def kernel(x, ffn1_norm, ffn1_w_gate, ffn1_w_up, ffn1_w_down, mix_norm, w_in, b_gate, mla_cq_norm, mla_ckv_norm, mla_w_uq, mla_w_ukv, mla_q_norm, mla_k_norm, mla_w_o, conv_w_dw, conv_b_dw, conv_ln_g, conv_ln_b, conv_w_pw, moba_q_norm, moba_k_norm, moba_w_o, w_out, ffn2_norm, ffn2_w_gate, ffn2_w_up, ffn2_w_down):
    cos, sin = _rope_tables()
    slopes = jnp.exp2(-8.0 * jnp.arange(1, MOBA_HEADS + 1, dtype=F32) / MOBA_HEADS)
    slopes = jnp.broadcast_to(slopes[:, None, None], (MOBA_HEADS, 1, 128))
    ffn1 = [w.astype(BF16) for w in (ffn1_w_gate, ffn1_w_up, ffn1_w_down)]
    ffn2 = [w.astype(BF16) for w in (ffn2_w_gate, ffn2_w_up, ffn2_w_down)]
    xt = x.reshape(TOKENS, D_MODEL)
    for l in range(DEPTH):
        xt = _ffn(xt, ffn1_norm[l].reshape(1, -1), *ffn1, l)
        gmix = mix_norm[l].reshape(1, -1)
        w_in_l = w_in[l]
        q, k, v = _mla_proj(xt, gmix, w_in_l, mla_cq_norm[l], mla_ckv_norm[l], mla_w_uq[l], mla_w_ukv[l],
                            mla_q_norm[l], mla_k_norm[l], cos, sin)
        o_mla = _mla_attn(q, k, v).reshape(TOKENS, MLA_HEADS * MLA_V)
        z = _glu_proj(xt, gmix, w_in_l)
        o_conv = _conv(z, conv_w_dw[l], conv_b_dw[l], conv_ln_g[l], conv_ln_b[l])
        qkv = _moba_proj(xt, gmix, w_in_l, moba_q_norm[l], moba_k_norm[l])
        o_moba = _moba_attn(qkv, slopes).reshape(TOKENS, MOBA_HEADS * MOBA_HEAD_DIM)
        xt = _merge(xt, gmix, o_mla, o_conv, o_moba, w_in_l, b_gate[l],
                    mla_w_o[l], conv_w_pw[l], moba_w_o[l], w_out[l])
        xt = _ffn(xt, ffn2_norm[l].reshape(1, -1), *ffn2, l)
    return xt.reshape(BATCH, SEQ, D_MODEL)
```

```python
import functools
import math

import jax
import jax.numpy as jnp
from jax import lax
from jax.experimental import pallas as pl
from jax.experimental.pallas import tpu as pltpu

D_MODEL = 2048
BATCH = 4
SEQ = 2048
DEPTH = 2
TOKENS = BATCH * SEQ

MLA_HEADS = 8
MLA_Q_RANK = 768
MLA_KV_RANK = 512
MLA_NOPE = 128
MLA_ROPE = 64
MLA_QK = MLA_NOPE + MLA_ROPE
MLA_V = 128
ROPE_THETA = 10000.0

CONV_CH = 1024
CONV_WIDTH = 31
CONV_HALO = 32

MOBA_HEADS = 8
MOBA_HEAD_DIM = 128
MOBA_BLOCK = 256
MOBA_TOPK = 3
MOBA_NB = SEQ // MOBA_BLOCK

D_FF = 5632
N_BRANCH = 3
NORM_EPS = 1e-6
NEG_INF = -1e30

OFF_CQ = 0
OFF_CKV = OFF_CQ + MLA_Q_RANK
OFF_KR = OFF_CKV + MLA_KV_RANK
OFF_CONV = OFF_KR + MLA_ROPE
OFF_MOBA = OFF_CONV + 2 * CONV_CH
OFF_GATE = OFF_MOBA + 3 * MOBA_HEADS * MOBA_HEAD_DIM

F32 = jnp.float32
BF16 = jnp.bfloat16

VMEM_LIMIT_BYTES = 56 * 1024 * 1024

TM_FFN = 512
TF_FFN = 512
TM_PROJ = 512
TN_CONV = 512
TM_MERGE = 512
TN_MERGE = 256
TQ_MLA = 256
TS_CONV = 256
CONV_ROWS = 32


def _params(*semantics):
    return pltpu.CompilerParams(dimension_semantics=semantics, vmem_limit_bytes=VMEM_LIMIT_BYTES)


def _rms_bf16(x, g):
    ms = jnp.mean(x * x, axis=-1, keepdims=True)
    return (x * lax.rsqrt(ms + NORM_EPS) * g).astype(BF16)


def _dot(a, b):
    return jnp.dot(a, b, preferred_element_type=F32)


def _dot_nt(a, b):
    return lax.dot_general(a, b, (((1,), (1,)), ((), ())), preferred_element_type=F32)


def _ffn_body(x_ref, g_ref, wg_ref, wu_ref, wd_ref, o_ref, h_scr, acc_scr):
    j = pl.program_id(1)

    @pl.when(j == 0)
    def _():
        h_scr[...] = _rms_bf16(x_ref[...], g_ref[...])

    h = h_scr[...]
    a = _dot(h, wg_ref[...])
    b = _dot(h, wu_ref[...])
    t = (a * jax.nn.sigmoid(a) * b).astype(BF16)
    p = _dot(t, wd_ref[...])

    @pl.when(j == 0)
    def _():
        acc_scr[...] = p

    @pl.when(j > 0)
    def _():
        acc_scr[...] += p

    @pl.when(j == pl.num_programs(1) - 1)
    def _():
        o_ref[...] = x_ref[...] + 0.5 * acc_scr[...]


def _ffn(x, g, wg, wu, wd, l):
    tm, tf = TM_FFN, TF_FFN
    return pl.pallas_call(
        _ffn_body,
        grid=(TOKENS // tm, D_FF // tf),
        in_specs=[
            pl.BlockSpec((tm, D_MODEL), lambda i, j: (i, 0)),
            pl.BlockSpec((1, D_MODEL), lambda i, j: (0, 0)),
            pl.BlockSpec((None, D_MODEL, tf), lambda i, j: (l, 0, j)),
            pl.BlockSpec((None, D_MODEL, tf), lambda i, j: (l, 0, j)),
            pl.BlockSpec((None, tf, D_MODEL), lambda i, j: (l, j, 0)),
        ],
        out_specs=pl.BlockSpec((tm, D_MODEL), lambda i, j: (i, 0)),
        out_shape=jax.ShapeDtypeStruct((TOKENS, D_MODEL), F32),
        scratch_shapes=[pltpu.VMEM((tm, D_MODEL), BF16), pltpu.VMEM((tm, D_MODEL), F32)],
        compiler_params=_params("parallel", "arbitrary"),
        name="ffn",
    )(x, g, wg, wu, wd)


def _mla_proj_body(x_ref, gmix_ref, wc_ref, wkr_ref, wkrr_ref, gcq_ref, gckv_ref,
                   wqn_ref, wqr_ref, wqrr_ref, wkn_ref, wv_ref,
                   gqn_ref, gqr_ref, gqrr_ref, gkn_ref, gkr_ref, gkrr_ref,
                   cos_ref, sin_ref, q_ref, k_ref, v_ref):
    hn = _rms_bf16(x_ref[...], gmix_ref[...])
    c = _dot(hn, wc_ref[...])
    cq = c[:, :MLA_Q_RANK]
    ckv = c[:, MLA_Q_RANK:]
    kr = _dot(hn, wkr_ref[...])
    krr = _dot(hn, wkrr_ref[...])
    cqn = _rms_bf16(cq, gcq_ref[...])
    ckvn = _rms_bf16(ckv, gckv_ref[...])
    cos = cos_ref[...]
    sin = sin_ref[...]
    kr_ss = jnp.sum(kr * kr, axis=-1, keepdims=True)
    kr_rot = kr * gkr_ref[...] * cos + krr * gkrr_ref[...] * sin
    q_scale = MLA_QK ** -0.5
    for h in range(MLA_HEADS):
        qn = _dot(cqn, wqn_ref[h])
        qr = _dot(cqn, wqr_ref[h])
        qrr = _dot(cqn, wqrr_ref[h])
        ss = (jnp.sum(qn * qn, axis=-1, keepdims=True) + jnp.sum(qr * qr, axis=-1, keepdims=True)) / MLA_QK
        r = lax.rsqrt(ss + NORM_EPS) * q_scale
        q_ref[0, h, :, :MLA_NOPE] = (qn * r * gqn_ref[...]).astype(BF16)
        q_ref[0, h, :, MLA_NOPE:] = ((qr * gqr_ref[...] * cos + qrr * gqrr_ref[...] * sin) * r).astype(BF16)
        kn = _dot(ckvn, wkn_ref[h])
        ss = (jnp.sum(kn * kn, axis=-1, keepdims=True) + kr_ss) / MLA_QK
        r = lax.rsqrt(ss + NORM_EPS)
        k_ref[0, h, :, :MLA_NOPE] = (kn * r * gkn_ref[...]).astype(BF16)
        k_ref[0, h, :, MLA_NOPE:] = (kr_rot * r).astype(BF16)
        v_ref[0, h] = _dot(ckvn, wv_ref[h]).astype(BF16)


def _rot_cols(w):
    half = w.shape[-1] // 2
    return jnp.concatenate([-w[..., half:], w[..., :half]], axis=-1)


def _swap_halves(g):
    half = g.shape[-1] // 2
    return jnp.concatenate([g[..., half:], g[..., :half]], axis=-1)


def _mla_proj(x, gmix, w_in_l, gcq, gckv, w_uq, w_ukv, gq, gk, cos, sin):
    tm = TM_PROJ
    n_s = SEQ // tm
    wc = w_in_l[:, OFF_CQ:OFF_KR].astype(BF16)
    wkr_f = w_in_l[:, OFF_KR:OFF_CONV]
    wkr = wkr_f.astype(BF16)
    wkrr = _rot_cols(wkr_f).astype(BF16)
    wq = w_uq.reshape(MLA_Q_RANK, MLA_HEADS, MLA_QK).transpose(1, 0, 2)
    wqn = wq[..., :MLA_NOPE].astype(BF16)
    wqr = wq[..., MLA_NOPE:].astype(BF16)
    wqrr = _rot_cols(wq[..., MLA_NOPE:]).astype(BF16)
    wkv = w_ukv.reshape(MLA_KV_RANK, MLA_HEADS, MLA_NOPE + MLA_V).transpose(1, 0, 2)
    wkn = wkv[..., :MLA_NOPE].astype(BF16)
    wv = wkv[..., MLA_NOPE:].astype(BF16)
    gq = gq.reshape(1, MLA_QK)
    gk = gk.reshape(1, MLA_QK)
    gqn, gqr, gqrr = gq[:, :MLA_NOPE], gq[:, MLA_NOPE:], _swap_halves(gq[:, MLA_NOPE:])
    gkn, gkr, gkrr = gk[:, :MLA_NOPE], gk[:, MLA_NOPE:], _swap_halves(gk[:, MLA_NOPE:])

    def full(a):
        nd = a.ndim
        return pl.BlockSpec(a.shape, lambda i: (0,) * nd)

    ins = [x, gmix, wc, wkr, wkrr, gcq.reshape(1, -1), gckv.reshape(1, -1), wqn, wqr, wqrr, wkn, wv,
           gqn, gqr, gqrr, gkn, gkr, gkrr, cos, sin]
    in_specs = [pl.BlockSpec((tm, D_MODEL), lambda i: (i, 0))] + [full(a) for a in ins[1:-2]]
    in_specs += [pl.BlockSpec((tm, MLA_ROPE), lambda i: (i % n_s, 0))] * 2
    head_spec = lambda d: pl.BlockSpec((1, MLA_HEADS, tm, d), lambda i: (i // n_s, 0, i % n_s, 0))
    return pl.pallas_call(
        _mla_proj_body,
        grid=(TOKENS // tm,),
        in_specs=in_specs,
        out_specs=[head_spec(MLA_QK), head_spec(MLA_QK), head_spec(MLA_V)],
        out_shape=[jax.ShapeDtypeStruct((BATCH, MLA_HEADS, SEQ, MLA_QK), BF16),
                   jax.ShapeDtypeStruct((BATCH, MLA_HEADS, SEQ, MLA_QK), BF16),
                   jax.ShapeDtypeStruct((BATCH, MLA_HEADS, SEQ, MLA_V), BF16)],
        compiler_params=_params("parallel"),
        name="mla_proj",
    )(*ins)


N_ATTN_TILES = SEQ // TQ_MLA
N_SCORE_CHUNKS = N_ATTN_TILES * (N_ATTN_TILES + 1) // 2


def _attend_tile(qi, q, k_chunk, v_chunk, s_scr, bias_fn):
    t = TQ_MLA
    base = qi * (qi + 1) // 2
    row = lax.broadcasted_iota(jnp.int32, (t, t), 0)
    col = lax.broadcasted_iota(jnp.int32, (t, t), 1)
    mx = None
    for j in range(qi + 1):
        s = _dot_nt(q, k_chunk(j))
        if bias_fn is not None:
            s = bias_fn(j, s)
        if j == qi:
            s = jnp.where(col <= row, s, NEG_INF)
        s_scr[:, (base + j) * t:(base + j + 1) * t] = s
        mx = s if mx is None else jnp.maximum(mx, s)
    m = jnp.max(mx, axis=-1, keepdims=True)
    lsum = None
    acc = None
    for j in range(qi + 1):
        p = jnp.exp(s_scr[:, (base + j) * t:(base + j + 1) * t] - m)
        lsum = p if lsum is None else lsum + p
        pv = _dot(p.astype(BF16), v_chunk(j))
        acc = pv if acc is None else acc + pv
    return acc / jnp.sum(lsum, axis=-1, keepdims=True)


def _mla_attn_body(q_ref, k_ref, v_ref, o_ref, s_scr):
    t = TQ_MLA
    for qi in range(N_ATTN_TILES):
        o = _attend_tile(qi, q_ref[0, 0, qi * t:(qi + 1) * t, :],
                         lambda j: k_ref[0, 0, j * t:(j + 1) * t, :],
                         lambda j: v_ref[0, 0, j * t:(j + 1) * t, :], s_scr, None)
        o_ref[0, qi * t:(qi + 1) * t, :] = o.astype(BF16)


def _mla_attn(q, k, v):
    t = TQ_MLA
    head = lambda d: pl.BlockSpec((1, 1, SEQ, d), lambda b, h: (b, h, 0, 0))
    return pl.pallas_call(
        _mla_attn_body,
        grid=(BATCH, MLA_HEADS),
        in_specs=[head(MLA_QK), head(MLA_QK), head(MLA_V)],
        out_specs=pl.BlockSpec((1, SEQ, MLA_V), lambda b, h: (b, 0, h)),
        out_shape=jax.ShapeDtypeStruct((BATCH, SEQ, MLA_HEADS * MLA_V), BF16),
        scratch_shapes=[pltpu.VMEM((t, N_SCORE_CHUNKS * t), F32)],
        compiler_params=_params("parallel", "parallel"),
        name="mla_attn",
    )(q, k, v)


def _glu_proj_body(x_ref, gmix_ref, wa_ref, wg_ref, z_ref, h_scr):
    @pl.when(pl.program_id(1) == 0)
    def _():
        h_scr[...] = _rms_bf16(x_ref[...], gmix_ref[...])

    h = h_scr[...]
    a = _dot(h, wa_ref[...])
    g = _dot(h, wg_ref[...])
    z_ref[...] = a * jax.nn.sigmoid(g)


def _glu_proj(x, gmix, w_in_l):
    tm, tn = TM_PROJ, TN_CONV
    wa = w_in_l[:, OFF_CONV:OFF_CONV + CONV_CH].astype(BF16)
    wg = w_in_l[:, OFF_CONV + CONV_CH:OFF_MOBA].astype(BF16)
    return pl.pallas_call(
        _glu_proj_body,
        grid=(TOKENS // tm, CONV_CH // tn),
        in_specs=[
            pl.BlockSpec((tm, D_MODEL), lambda i, j: (i, 0)),
            pl.BlockSpec((1, D_MODEL), lambda i, j: (0, 0)),
            pl.BlockSpec((D_MODEL, tn), lambda i, j: (0, j)),
            pl.BlockSpec((D_MODEL, tn), lambda i, j: (0, j)),
        ],
        out_specs=pl.BlockSpec((tm, tn), lambda i, j: (i, j)),
        out_shape=jax.ShapeDtypeStruct((TOKENS, CONV_CH), F32),
        scratch_shapes=[pltpu.VMEM((tm, D_MODEL), BF16)],
        compiler_params=_params("parallel", "arbitrary"),
        name="glu_proj",
    )(x, gmix, wa, wg)


def _conv_body(z_ref, w_ref, b_ref, lg_ref, lb_ref, o_ref, zbuf):
    ts, halo = TS_CONV, CONV_HALO
    si = pl.program_id(1)

    @pl.when(si == 0)
    def _():
        zbuf[0:halo, :] = jnp.zeros((halo, CONV_CH), F32)

    @pl.when(si > 0)
    def _():
        zbuf[0:halo, :] = zbuf[ts:ts + halo, :]

    zbuf[halo:halo + ts, :] = z_ref[...]
    shift = halo - (CONV_WIDTH - 1)
    for r0 in range(0, ts, CONV_ROWS):
        acc = jnp.zeros((CONV_ROWS, CONV_CH), F32) + b_ref[...]
        for k in range(CONV_WIDTH):
            acc = acc + zbuf[r0 + shift + k:r0 + shift + k + CONV_ROWS, :] * w_ref[k:k + 1, :]
        mu = jnp.mean(acc, axis=-1, keepdims=True)
        xc = acc - mu
        var = jnp.mean(xc * xc, axis=-1, keepdims=True)
        y = xc * lax.rsqrt(var + NORM_EPS) * lg_ref[...] + lb_ref[...]
        o_ref[r0:r0 + CONV_ROWS, :] = (y * jax.nn.sigmoid(y)).astype(BF16)


def _conv(z, w_dw, b_dw, ln_g, ln_b):
    ts = TS_CONV
    n_s = SEQ // ts
    vec = pl.BlockSpec((1, CONV_CH), lambda b, s: (0, 0))
    return pl.pallas_call(
        _conv_body,
        grid=(BATCH, n_s),
        in_specs=[
            pl.BlockSpec((ts, CONV_CH), lambda b, s: (b * n_s + s, 0)),
            pl.BlockSpec((CONV_WIDTH, CONV_CH), lambda b, s: (0, 0)),
            vec, vec, vec,
        ],
        out_specs=pl.BlockSpec((ts, CONV_CH), lambda b, s: (b * n_s + s, 0)),
        out_shape=jax.ShapeDtypeStruct((TOKENS, CONV_CH), BF16),
        scratch_shapes=[pltpu.VMEM((ts + CONV_HALO, CONV_CH), F32)],
        compiler_params=_params("arbitrary", "arbitrary"),
        name="conv",
    )(z, w_dw, b_dw.reshape(1, -1), ln_g.reshape(1, -1), ln_b.reshape(1, -1))


def _moba_proj_body(x_ref, gmix_ref, w_ref, g_ref, o_ref, h_scr):
    j = pl.program_id(1)

    @pl.when(j == 0)
    def _():
        h_scr[...] = _rms_bf16(x_ref[...], gmix_ref[...])

    u = _dot(h_scr[...], w_ref[...])
    g = g_ref[0]
    for h in range(MOBA_HEADS):
        uh = u[:, h * MOBA_HEAD_DIM:(h + 1) * MOBA_HEAD_DIM]
        ms = jnp.mean(uh * uh, axis=-1, keepdims=True)
        normed = uh * lax.rsqrt(ms + NORM_EPS) * g
        o_ref[0, 0, h] = jnp.where(j < 2, normed, uh)


def _moba_proj(x, gmix, w_in_l, gq, gk):
    tm = TM_PROJ
    n_s = SEQ // tm
    hd = MOBA_HEADS * MOBA_HEAD_DIM
    w = w_in_l[:, OFF_MOBA:OFF_GATE].astype(BF16)
    g = jnp.stack([gq, gk, jnp.ones_like(gq)]).reshape(3, 1, MOBA_HEAD_DIM)
    return pl.pallas_call(
        _moba_proj_body,
        grid=(TOKENS // tm, 3),
        in_specs=[
            pl.BlockSpec((tm, D_MODEL), lambda i, j: (i, 0)),
            pl.BlockSpec((1, D_MODEL), lambda i, j: (0, 0)),
            pl.BlockSpec((D_MODEL, hd), lambda i, j: (0, j)),
            pl.BlockSpec((1, 1, MOBA_HEAD_DIM), lambda i, j: (j, 0, 0)),
        ],
        out_specs=pl.BlockSpec((1, 1, MOBA_HEADS, tm, MOBA_HEAD_DIM),
                               lambda i, j: (j, i // n_s, 0, i % n_s, 0)),
        out_shape=jax.ShapeDtypeStruct((3, BATCH, MOBA_HEADS, SEQ, MOBA_HEAD_DIM), F32),
        scratch_shapes=[pltpu.VMEM((tm, D_MODEL), BF16)],
        compiler_params=_params("parallel", "arbitrary"),
        name="moba_proj",
    )(x, gmix, w, g)


def _moba_attn_body(slope_ref, q_ref, k_ref, v_ref, o_ref, kb_scr, vb_scr, s_scr):
    blk, nb = MOBA_BLOCK, MOBA_NB
    kb_scr[...] = k_ref[0, 0, 0].astype(BF16)
    vb_scr[...] = v_ref[0, 0, 0].astype(BF16)
    km = jnp.concatenate(
        [jnp.mean(k_ref[0, 0, 0, j * blk:(j + 1) * blk, :], axis=0, keepdims=True) for j in range(nb)], axis=0)
    slope = slope_ref[0][:, 0:1]
    row = lax.broadcasted_iota(jnp.int32, (blk, blk), 0)
    col = lax.broadcasted_iota(jnp.int32, (blk, blk), 1)
    rel = (row - col).astype(F32)
    lane = lax.broadcasted_iota(jnp.int32, (blk, nb), 1)

    for n in range(nb):
        q = q_ref[0, 0, 0, n * blk:(n + 1) * blk, :]
        row_bias = None
        if n > MOBA_TOPK:
            cols = []
            gate = jnp.full((blk, nb), NEG_INF, F32)
            for j in range(nb):
                if j < n:
                    gj = jnp.sum(q * km[j:j + 1, :], axis=-1, keepdims=True)
                    gate = jnp.where(lane == j, gj, gate)
                else:
                    gj = jnp.full((blk, 1), NEG_INF, F32)
                cols.append(gj)
            rank = jnp.zeros((blk, nb), jnp.int32)
            for j in range(nb):
                beats = (cols[j] > gate) | ((cols[j] == gate) & (lane > j))
                rank = rank + jnp.where(beats, 1, 0)
            row_bias = jnp.where((rank < MOBA_TOPK) & (lane < n), 0.0, NEG_INF)

        def bias_fn(j, s, n=n, row_bias=row_bias):
            s = s - slope * (rel + float((n - j) * blk))
            if row_bias is not None and j < n:
                s = s + row_bias[:, j:j + 1]
            return s

        qb = (q * (MOBA_HEAD_DIM ** -0.5)).astype(BF16)
        o = _attend_tile(n, qb, lambda j: kb_scr[j * blk:(j + 1) * blk, :],
                         lambda j: vb_scr[j * blk:(j + 1) * blk, :], s_scr, bias_fn)
        o_ref[0, n * blk:(n + 1) * blk, :] = o.astype(BF16)


def _moba_attn(qkv, slopes):
    blk = MOBA_BLOCK
    assert blk == TQ_MLA
    spec = lambda which: pl.BlockSpec((1, 1, 1, SEQ, MOBA_HEAD_DIM), lambda b, h: (which, b, h, 0, 0))
    return pl.pallas_call(
        _moba_attn_body,
        grid=(BATCH, MOBA_HEADS),
        in_specs=[pl.BlockSpec((1, 1, 128), lambda b, h: (h, 0, 0)), spec(0), spec(1), spec(2)],
        out_specs=pl.BlockSpec((1, SEQ, MOBA_HEAD_DIM), lambda b, h: (b, 0, h)),
        out_shape=jax.ShapeDtypeStruct((BATCH, SEQ, MOBA_HEADS * MOBA_HEAD_DIM), BF16),
        scratch_shapes=[pltpu.VMEM((SEQ, MOBA_HEAD_DIM), BF16), pltpu.VMEM((SEQ, MOBA_HEAD_DIM), BF16),
                        pltpu.VMEM((blk, N_SCORE_CHUNKS * blk), F32)],
        compiler_params=_params("parallel", "parallel"),
        name="moba_attn",
    )(slopes, qkv, qkv, qkv)


def _merge_body(x_ref, gmix_ref, oa_ref, ob_ref, oc_ref, wg_ref, bg_ref, woa_ref, wob_ref, woc_ref,
                wout_ref, o_ref, h_scr, acc_scr):
    j = pl.program_id(1)

    @pl.when(j == 0)
    def _():
        h_scr[...] = _rms_bf16(x_ref[...], gmix_ref[...])

    h = h_scr[...]
    merged = None
    for i, (o_i, w_i) in enumerate(((oa_ref, woa_ref), (ob_ref, wob_ref), (oc_ref, woc_ref))):
        gate = jax.nn.sigmoid(_dot(h, wg_ref[i]) + bg_ref[i])
        term = gate * _dot(o_i[...], w_i[...])
        merged = term if merged is None else merged + term
    p = _dot(merged.astype(BF16), wout_ref[...])

    @pl.when(j == 0)
    def _():
        acc_scr[...] = p

    @pl.when(j > 0)
    def _():
        acc_scr[...] += p

    @pl.when(j == pl.num_programs(1) - 1)
    def _():
        o_ref[...] = x_ref[...] + acc_scr[...]


def _merge(x, gmix, oa, ob, oc, w_in_l, b_gate, w_oa, w_ob, w_oc, w_out):
    tm, tn = TM_MERGE, TN_MERGE
    wg = w_in_l[:, OFF_GATE:].astype(BF16).reshape(D_MODEL, N_BRANCH, D_MODEL).transpose(1, 0, 2)
    bg = b_gate.reshape(N_BRANCH, 1, D_MODEL)
    tok = lambda d: pl.BlockSpec((tm, d), lambda i, j: (i, 0))
    wo = lambda d: pl.BlockSpec((d, tn), lambda i, j: (0, j))
    return pl.pallas_call(
        _merge_body,
        grid=(TOKENS // tm, D_MODEL // tn),
        in_specs=[
            tok(D_MODEL),
            pl.BlockSpec((1, D_MODEL), lambda i, j: (0, 0)),
            tok(MLA_HEADS * MLA_V), tok(CONV_CH), tok(MOBA_HEADS * MOBA_HEAD_DIM),
            pl.BlockSpec((N_BRANCH, D_MODEL, tn), lambda i, j: (0, 0, j)),
            pl.BlockSpec((N_BRANCH, 1, tn), lambda i, j: (0, 0, j)),
            wo(MLA_HEADS * MLA_V), wo(CONV_CH), wo(MOBA_HEADS * MOBA_HEAD_DIM),
            pl.BlockSpec((tn, D_MODEL), lambda i, j: (j, 0)),
        ],
        out_specs=tok(D_MODEL),
        out_shape=jax.ShapeDtypeStruct((TOKENS, D_MODEL), F32),
        scratch_shapes=[pltpu.VMEM((tm, D_MODEL), BF16), pltpu.VMEM((tm, D_MODEL), F32)],
        compiler_params=_params("parallel", "arbitrary"),
        name="merge",
    )(x, gmix, oa, ob, oc, wg, bg, w_oa.astype(BF16), w_ob.astype(BF16), w_oc.astype(BF16),
      w_out.astype(BF16))


def _rope_tables():
    half = MLA_ROPE // 2
    inv_freq = jnp.exp(-math.log(ROPE_THETA) * jnp.arange(half, dtype=F32) * 2.0 / MLA_ROPE)
    ang = jnp.arange(SEQ, dtype=jnp.int32).astype(F32)[:, None] * inv_freq[None, :]
    cos, sin = jnp.cos(ang), jnp.sin(ang)
    return jnp.concatenate([cos, cos], axis=-1), jnp.concatenate([sin, sin], axis=-1)


def kernel(x, ffn1_norm, ffn1_w_gate, ffn1_w_up, ffn1_w_down, mix_norm, w_in, b_gate, mla_cq_norm, mla_ckv_norm, mla_w_uq, mla_w_ukv, mla_q_norm, mla_k_norm, mla_w_o, conv_w_dw, conv_b_dw, conv_ln_g, conv_ln_b, conv_w_pw, moba_q_norm, moba_k_norm, moba_w_o, w_out, ffn2_norm, ffn2_w_gate, ffn2_w_up, ffn2_w_down):
    cos, sin = _rope_tables()
    slopes = jnp.exp2(-8.0 * jnp.arange(1, MOBA_HEADS + 1, dtype=F32) / MOBA_HEADS)
    slopes = jnp.broadcast_to(slopes[:, None, None], (MOBA_HEADS, 1, 128))
    ffn1 = [w.astype(BF16) for w in (ffn1_w_gate, ffn1_w_up, ffn1_w_down)]
    ffn2 = [w.astype(BF16) for w in (ffn2_w_gate, ffn2_w_up, ffn2_w_down)]
    xt = x.reshape(TOKENS, D_MODEL)
    for l in range(DEPTH):
        xt = _ffn(xt, ffn1_norm[l].reshape(1, -1), *ffn1, l)
        gmix = mix_norm[l].reshape(1, -1)
        w_in_l = w_in[l]
        q, k, v = _mla_proj(xt, gmix, w_in_l, mla_cq_norm[l], mla_ckv_norm[l], mla_w_uq[l], mla_w_ukv[l],
                            mla_q_norm[l], mla_k_norm[l], cos, sin)
        o_mla = _mla_attn(q, k, v).reshape(TOKENS, MLA_HEADS * MLA_V)
        z = _glu_proj(xt, gmix, w_in_l)
        o_conv = _conv(z, conv_w_dw[l], conv_b_dw[l], conv_ln_g[l], conv_ln_b[l])
        qkv = _moba_proj(xt, gmix, w_in_l, moba_q_norm[l], moba_k_norm[l])
        o_moba = _moba_attn(qkv, slopes).reshape(TOKENS, MOBA_HEADS * MOBA_HEAD_DIM)
        xt = _merge(xt, gmix, o_mla, o_conv, o_moba, w_in_l, b_gate[l],
                    mla_w_o[l], conv_w_pw[l], moba_w_o[l], w_out[l])
        xt = _ffn(xt, ffn2_norm[l].reshape(1, -1), *ffn2, l)
    return xt.reshape(BATCH, SEQ, D_MODEL)
```

```python
import math

import jax
import jax.numpy as jnp
from jax import lax
from jax.experimental import pallas as pl
from jax.experimental.pallas import tpu as pltpu

D_MODEL = 2048
BATCH = 4
SEQ = 2048
DEPTH = 2
TOKENS = BATCH * SEQ

MLA_HEADS = 8
MLA_Q_RANK = 768
MLA_KV_RANK = 512
MLA_NOPE = 128
MLA_ROPE = 64
MLA_QK = MLA_NOPE + MLA_ROPE
MLA_V = 128
ROPE_THETA = 10000.0

CONV_CH = 1024
CONV_WIDTH = 31
CONV_HALO = 32

MOBA_HEADS = 8
MOBA_HEAD_DIM = 128
MOBA_BLOCK = 256
MOBA_TOPK = 3
MOBA_NB = SEQ // MOBA_BLOCK

D_FF = 5632
N_BRANCH = 3
NORM_EPS = 1e-6
NEG_INF = -1e30

OFF_CQ = 0
OFF_CKV = OFF_CQ + MLA_Q_RANK
OFF_KR = OFF_CKV + MLA_KV_RANK
OFF_CONV = OFF_KR + MLA_ROPE
OFF_MOBA = OFF_CONV + 2 * CONV_CH
OFF_GATE = OFF_MOBA + 3 * MOBA_HEADS * MOBA_HEAD_DIM

MOBA_COLS = 3 * MOBA_HEADS * MOBA_HEAD_DIM
MLA_LATENT = MLA_Q_RANK + MLA_KV_RANK
COL_MOBA = 0
COL_GATE = COL_MOBA + MOBA_COLS
COL_CONV_A = COL_GATE + N_BRANCH * D_MODEL
COL_CONV_G = COL_CONV_A + CONV_CH
COL_LATENT = -(-(COL_CONV_G + CONV_CH) // MLA_LATENT) * MLA_LATENT
COL_KR = COL_LATENT + MLA_LATENT
W_IN_COLS = COL_KR + 2 * MLA_ROPE

F32 = jnp.float32
BF16 = jnp.bfloat16

VMEM_LIMIT_BYTES = 56 * 1024 * 1024

TM_FFN = 512
TF_FFN = 512
TM_PROJ = 512
TN_CONV = 512
TM_MERGE = 512
TN_MERGE = 512
TQ_MLA = 256
TS_CONV = 256
CONV_ROWS = 32


def _params(*semantics):
    return pltpu.CompilerParams(dimension_semantics=semantics, vmem_limit_bytes=VMEM_LIMIT_BYTES)


def _rms_bf16(x, g):
    ms = jnp.mean(x * x, axis=-1, keepdims=True)
    return (x * lax.rsqrt(ms + NORM_EPS) * g).astype(BF16)


def _dot(a, b):
    return jnp.dot(a, b, preferred_element_type=F32)


def _dot_nt(a, b):
    return lax.dot_general(a, b, (((1,), (1,)), ((), ())), preferred_element_type=F32)


def _rot_cols(w):
    half = w.shape[-1] // 2
    return jnp.concatenate([-w[..., half:], w[..., :half]], axis=-1)


def _swap_halves(g):
    half = g.shape[-1] // 2
    return jnp.concatenate([g[..., half:], g[..., :half]], axis=-1)


def _prep_w_in(w_in):
    kr = w_in[..., OFF_KR:OFF_CONV]
    gap = jnp.zeros(w_in.shape[:-1] + (COL_LATENT - COL_CONV_G - CONV_CH,), w_in.dtype)
    cols = [w_in[..., OFF_MOBA:OFF_GATE], w_in[..., OFF_GATE:], w_in[..., OFF_CONV:OFF_MOBA], gap,
            w_in[..., OFF_CQ:OFF_KR], kr, _rot_cols(kr)]
    return jnp.concatenate(cols, axis=-1).astype(BF16)


def _ffn_body(x_ref, g_ref, wg_ref, wu_ref, wd_ref, o_ref, h_scr, acc_scr):
    j = pl.program_id(1)

    @pl.when(j == 0)
    def _():
        h_scr[...] = _rms_bf16(x_ref[...], g_ref[...])
        acc_scr[...] = jnp.zeros_like(acc_scr)

    h = h_scr[...]
    a = _dot(h, wg_ref[...])
    b = _dot(h, wu_ref[...])
    t = (a * jax.nn.sigmoid(a) * b).astype(BF16)
    acc_scr[...] += _dot(t, wd_ref[...])

    @pl.when(j == pl.num_programs(1) - 1)
    def _():
        o_ref[...] = x_ref[...] + 0.5 * acc_scr[...]


def _ffn(x, g, wg, wu, wd, l):
    tm, tf = TM_FFN, TF_FFN
    return pl.pallas_call(
        _ffn_body,
        grid=(TOKENS // tm, D_FF // tf),
        in_specs=[
            pl.BlockSpec((tm, D_MODEL), lambda i, j: (i, 0)),
            pl.BlockSpec((1, D_MODEL), lambda i, j: (0, 0)),
            pl.BlockSpec((None, D_MODEL, tf), lambda i, j: (l, 0, j)),
            pl.BlockSpec((None, D_MODEL, tf), lambda i, j: (l, 0, j)),
            pl.BlockSpec((None, tf, D_MODEL), lambda i, j: (l, j, 0)),
        ],
        out_specs=pl.BlockSpec((tm, D_MODEL), lambda i, j: (i, 0)),
        out_shape=jax.ShapeDtypeStruct((TOKENS, D_MODEL), F32),
        scratch_shapes=[pltpu.VMEM((tm, D_MODEL), BF16), pltpu.VMEM((tm, D_MODEL), F32)],
        compiler_params=_params("parallel", "arbitrary"),
        name="ffn",
    )(x, g, wg, wu, wd)


def _rope_pair(t, table):
    u = t * table
    return (u + pltpu.roll(u, MLA_ROPE, 1))[:, :MLA_ROPE]


def _mla_proj_body(x_ref, gmix_ref, wc_ref, wkk_ref, gcq_ref, gckv_ref, wq_ref, wkv_ref,
                   gqn_ref, g2q_ref, gkn_ref, g2k_ref, cs_ref, q_ref, k_ref, v_ref):
    tm = x_ref.shape[0]
    hn = _rms_bf16(x_ref[...], gmix_ref[...])
    c = _dot(hn, wc_ref[...])
    cqn = _rms_bf16(c[:, :MLA_Q_RANK], gcq_ref[...])
    ckvn = _rms_bf16(c[:, MLA_Q_RANK:], gckv_ref[...])
    kk = _dot(hn, wkk_ref[...])
    low = lax.broadcasted_iota(jnp.int32, (tm, 2 * MLA_ROPE), 1) < MLA_ROPE
    cs = cs_ref[...]
    kr_ss = jnp.sum(jnp.where(low, kk * kk, 0.0), axis=-1, keepdims=True)
    kr_rot = _rope_pair(kk, cs * g2k_ref[...])
    cs_q = cs * g2q_ref[...]
    q_scale = MLA_QK ** -0.5
    hw = MLA_NOPE + 2 * MLA_ROPE
    for h in range(MLA_HEADS):
        rq = _dot(cqn, wq_ref[:, h * hw:(h + 1) * hw])
        qn, qt = rq[:, :MLA_NOPE], rq[:, MLA_NOPE:]
        ss = (jnp.sum(qn * qn, axis=-1, keepdims=True)
              + jnp.sum(jnp.where(low, qt * qt, 0.0), axis=-1, keepdims=True)) / MLA_QK
        r = lax.rsqrt(ss + NORM_EPS) * q_scale
        q_ref[0, h, :, :MLA_NOPE] = (qn * r * gqn_ref[...]).astype(BF16)
        q_ref[0, h, :, MLA_NOPE:] = (_rope_pair(qt, cs_q) * r).astype(BF16)
        rkv = _dot(ckvn, wkv_ref[:, h * hw:(h + 1) * hw])
        kn = rkv[:, :MLA_NOPE]
        ss = (jnp.sum(kn * kn, axis=-1, keepdims=True) + kr_ss) / MLA_QK
        r = lax.rsqrt(ss + NORM_EPS)
        k_ref[0, h, :, :MLA_NOPE] = (kn * r * gkn_ref[...]).astype(BF16)
        k_ref[0, h, :, MLA_NOPE:] = (kr_rot * r).astype(BF16)
        v_ref[0, h] = rkv[:, MLA_NOPE:].astype(BF16)


def _mla_proj(x, gmix, w_in_b, l, gcq, gckv, wq, wkv, gq, gk, cs):
    tm = TM_PROJ
    n_s = SEQ // tm
    gq = gq.reshape(1, MLA_QK)
    gk = gk.reshape(1, MLA_QK)
    pair = lambda g: jnp.concatenate([g[:, MLA_NOPE:], _swap_halves(g[:, MLA_NOPE:])], axis=-1)
    vec = lambda n: pl.BlockSpec((1, n), lambda i: (0, 0))
    head_spec = lambda d: pl.BlockSpec((1, MLA_HEADS, tm, d), lambda i: (i // n_s, 0, i % n_s, 0))
    return pl.pallas_call(
        _mla_proj_body,
        grid=(TOKENS // tm,),
        in_specs=[
            pl.BlockSpec((tm, D_MODEL), lambda i: (i, 0)),
            vec(D_MODEL),
            pl.BlockSpec((None, D_MODEL, MLA_LATENT), lambda i: (l, 0, COL_LATENT // MLA_LATENT)),
            pl.BlockSpec((None, D_MODEL, 2 * MLA_ROPE), lambda i: (l, 0, COL_KR // (2 * MLA_ROPE))),
            vec(MLA_Q_RANK), vec(MLA_KV_RANK),
            pl.BlockSpec((None,) + wq.shape[1:], lambda i: (l, 0, 0)),
            pl.BlockSpec((None,) + wkv.shape[1:], lambda i: (l, 0, 0)),
            vec(MLA_NOPE), vec(2 * MLA_ROPE), vec(MLA_NOPE), vec(2 * MLA_ROPE),
            pl.BlockSpec((tm, 2 * MLA_ROPE), lambda i: (i % n_s, 0)),
        ],
        out_specs=[head_spec(MLA_QK), head_spec(MLA_QK), head_spec(MLA_V)],
        out_shape=[jax.ShapeDtypeStruct((BATCH, MLA_HEADS, SEQ, MLA_QK), BF16),
                   jax.ShapeDtypeStruct((BATCH, MLA_HEADS, SEQ, MLA_QK), BF16),
                   jax.ShapeDtypeStruct((BATCH, MLA_HEADS, SEQ, MLA_V), BF16)],
        compiler_params=_params("parallel"),
        name="mla_proj",
    )(x, gmix, w_in_b, w_in_b, gcq.reshape(1, -1), gckv.reshape(1, -1), wq, wkv,
      gq[:, :MLA_NOPE], pair(gq), gk[:, :MLA_NOPE], pair(gk), cs)


N_ATTN_TILES = SEQ // TQ_MLA
N_SCORE_CHUNKS = N_ATTN_TILES * (N_ATTN_TILES + 1) // 2


def _attend_tile(qi, q, k_chunk, v_chunk, s_scr, bias_fn):
    t = TQ_MLA
    base = qi * (qi + 1) // 2
    row = lax.broadcasted_iota(jnp.int32, (t, t), 0)
    col = lax.broadcasted_iota(jnp.int32, (t, t), 1)
    mx = None
    for j in range(qi + 1):
        s = _dot_nt(q, k_chunk(j))
        if bias_fn is not None:
            s = bias_fn(j, s)
        if j == qi:
            s = jnp.where(col <= row, s, NEG_INF)
        s_scr[:, (base + j) * t:(base + j + 1) * t] = s
        mx = s if mx is None else jnp.maximum(mx, s)
    m = jnp.max(mx, axis=-1, keepdims=True)
    lsum = None
    acc = None
    for j in range(qi + 1):
        p = jnp.exp(s_scr[:, (base + j) * t:(base + j + 1) * t] - m)
        lsum = p if lsum is None else lsum + p
        pv = _dot(p.astype(BF16), v_chunk(j))
        acc = pv if acc is None else acc + pv
    return acc / jnp.sum(lsum, axis=-1, keepdims=True)


def _mla_attn_body(q_ref, k_ref, v_ref, o_ref, s_scr):
    t = TQ_MLA
    for qi in range(N_ATTN_TILES):
        o = _attend_tile(qi, q_ref[0, 0, qi * t:(qi + 1) * t, :],
                         lambda j: k_ref[0, 0, j * t:(j + 1) * t, :],
                         lambda j: v_ref[0, 0, j * t:(j + 1) * t, :], s_scr, None)
        o_ref[0, qi * t:(qi + 1) * t, :] = o.astype(BF16)


def _mla_attn(q, k, v):
    t = TQ_MLA
    head = lambda d: pl.BlockSpec((1, 1, SEQ, d), lambda b, h: (b, h, 0, 0))
    return pl.pallas_call(
        _mla_attn_body,
        grid=(BATCH, MLA_HEADS),
        in_specs=[head(MLA_QK), head(MLA_QK), head(MLA_V)],
        out_specs=pl.BlockSpec((1, SEQ, MLA_V), lambda b, h: (b, 0, h)),
        out_shape=jax.ShapeDtypeStruct((BATCH, SEQ, MLA_HEADS * MLA_V), BF16),
        scratch_shapes=[pltpu.VMEM((t, N_SCORE_CHUNKS * t), F32)],
        compiler_params=_params("parallel", "parallel"),
        name="mla_attn",
    )(q, k, v)


def _glu_proj_body(x_ref, gmix_ref, wa_ref, wg_ref, z_ref, h_scr):
    @pl.when(pl.program_id(1) == 0)
    def _():
        h_scr[...] = _rms_bf16(x_ref[...], gmix_ref[...])

    h = h_scr[...]
    a = _dot(h, wa_ref[...])
    g = _dot(h, wg_ref[...])
    z_ref[...] = a * jax.nn.sigmoid(g)


def _glu_proj(x, gmix, w_in_b, l):
    tm, tn = TM_PROJ, TN_CONV
    return pl.pallas_call(
        _glu_proj_body,
        grid=(TOKENS // tm, CONV_CH // tn),
        in_specs=[
            pl.BlockSpec((tm, D_MODEL), lambda i, j: (i, 0)),
            pl.BlockSpec((1, D_MODEL), lambda i, j: (0, 0)),
            pl.BlockSpec((None, D_MODEL, tn), lambda i, j: (l, 0, COL_CONV_A // tn + j)),
            pl.BlockSpec((None, D_MODEL, tn), lambda i, j: (l, 0, COL_CONV_G // tn + j)),
        ],
        out_specs=pl.BlockSpec((tm, tn), lambda i, j: (i, j)),
        out_shape=jax.ShapeDtypeStruct((TOKENS, CONV_CH), F32),
        scratch_shapes=[pltpu.VMEM((tm, D_MODEL), BF16)],
        compiler_params=_params("parallel", "arbitrary"),
        name="glu_proj",
    )(x, gmix, w_in_b, w_in_b)


def _conv_body(z_ref, w_ref, b_ref, lg_ref, lb_ref, o_ref, zbuf):
    ts, halo = TS_CONV, CONV_HALO
    si = pl.program_id(1)

    @pl.when(si == 0)
    def _():
        zbuf[0:halo, :] = jnp.zeros((halo, CONV_CH), F32)

    @pl.when(si > 0)
    def _():
        zbuf[0:halo, :] = zbuf[ts:ts + halo, :]

    zbuf[halo:halo + ts, :] = z_ref[...]
    shift = halo - (CONV_WIDTH - 1)
    for r0 in range(0, ts, CONV_ROWS):
        acc = jnp.zeros((CONV_ROWS, CONV_CH), F32) + b_ref[...]
        for k in range(CONV_WIDTH):
            acc = acc + zbuf[r0 + shift + k:r0 + shift + k + CONV_ROWS, :] * w_ref[k:k + 1, :]
        mu = jnp.mean(acc, axis=-1, keepdims=True)
        xc = acc - mu
        var = jnp.mean(xc * xc, axis=-1, keepdims=True)
        y = xc * lax.rsqrt(var + NORM_EPS) * lg_ref[...] + lb_ref[...]
        o_ref[r0:r0 + CONV_ROWS, :] = (y * jax.nn.sigmoid(y)).astype(BF16)


def _conv(z, w_dw, b_dw, ln_g, ln_b):
    ts = TS_CONV
    n_s = SEQ // ts
    vec = pl.BlockSpec((1, CONV_CH), lambda b, s: (0, 0))
    return pl.pallas_call(
        _conv_body,
        grid=(BATCH, n_s),
        in_specs=[
            pl.BlockSpec((ts, CONV_CH), lambda b, s: (b * n_s + s, 0)),
            pl.BlockSpec((CONV_WIDTH, CONV_CH), lambda b, s: (0, 0)),
            vec, vec, vec,
        ],
        out_specs=pl.BlockSpec((ts, CONV_CH), lambda b, s: (b * n_s + s, 0)),
        out_shape=jax.ShapeDtypeStruct((TOKENS, CONV_CH), BF16),
        scratch_shapes=[pltpu.VMEM((ts + CONV_HALO, CONV_CH), F32)],
        compiler_params=_params("arbitrary", "arbitrary"),
        name="conv",
    )(z, w_dw, b_dw.reshape(1, -1), ln_g.reshape(1, -1), ln_b.reshape(1, -1))


def _moba_proj_body(x_ref, gmix_ref, w_ref, g_ref, o_ref, h_scr):
    j = pl.program_id(1)

    @pl.when(j == 0)
    def _():
        h_scr[...] = _rms_bf16(x_ref[...], gmix_ref[...])

    u = _dot(h_scr[...], w_ref[...])
    g = g_ref[0]
    for h in range(MOBA_HEADS):
        uh = u[:, h * MOBA_HEAD_DIM:(h + 1) * MOBA_HEAD_DIM]
        ms = jnp.mean(uh * uh, axis=-1, keepdims=True)
        normed = uh * lax.rsqrt(ms + NORM_EPS) * g
        o_ref[0, 0, h] = jnp.where(j < 2, normed, uh)


def _moba_proj(x, gmix, w_in_b, l, gq, gk):
    tm = TM_PROJ
    n_s = SEQ // tm
    hd = MOBA_HEADS * MOBA_HEAD_DIM
    g = jnp.stack([gq, gk, jnp.ones_like(gq)]).reshape(3, 1, MOBA_HEAD_DIM)
    return pl.pallas_call(
        _moba_proj_body,
        grid=(TOKENS // tm, 3),
        in_specs=[
            pl.BlockSpec((tm, D_MODEL), lambda i, j: (i, 0)),
            pl.BlockSpec((1, D_MODEL), lambda i, j: (0, 0)),
            pl.BlockSpec((None, D_MODEL, hd), lambda i, j: (l, 0, COL_MOBA // hd + j)),
            pl.BlockSpec((1, 1, MOBA_HEAD_DIM), lambda i, j: (j, 0, 0)),
        ],
        out_specs=pl.BlockSpec((1, 1, MOBA_HEADS, tm, MOBA_HEAD_DIM),
                               lambda i, j: (j, i // n_s, 0, i % n_s, 0)),
        out_shape=jax.ShapeDtypeStruct((3, BATCH, MOBA_HEADS, SEQ, MOBA_HEAD_DIM), F32),
        scratch_shapes=[pltpu.VMEM((tm, D_MODEL), BF16)],
        compiler_params=_params("parallel", "arbitrary"),
        name="moba_proj",
    )(x, gmix, w_in_b, g)


def _moba_attn_body(slope_ref, q_ref, k_ref, v_ref, o_ref, kb_scr, vb_scr, s_scr):
    blk, nb = MOBA_BLOCK, MOBA_NB
    kb_scr[...] = k_ref[0, 0, 0].astype(BF16)
    vb_scr[...] = v_ref[0, 0, 0].astype(BF16)
    km = jnp.concatenate(
        [jnp.mean(k_ref[0, 0, 0, j * blk:(j + 1) * blk, :], axis=0, keepdims=True) for j in range(nb)], axis=0)
    slope = slope_ref[0][:, 0:1]
    row = lax.broadcasted_iota(jnp.int32, (blk, blk), 0)
    col = lax.broadcasted_iota(jnp.int32, (blk, blk), 1)
    rel = (row - col).astype(F32)
    lane = lax.broadcasted_iota(jnp.int32, (blk, nb), 1)

    for n in range(nb):
        q = q_ref[0, 0, 0, n * blk:(n + 1) * blk, :]
        row_bias = None
        if n > MOBA_TOPK:
            cols = []
            gate = jnp.full((blk, nb), NEG_INF, F32)
            for j in range(nb):
                if j < n:
                    gj = jnp.sum(q * km[j:j + 1, :], axis=-1, keepdims=True)
                    gate = jnp.where(lane == j, gj, gate)
                else:
                    gj = jnp.full((blk, 1), NEG_INF, F32)
                cols.append(gj)
            rank = jnp.zeros((blk, nb), jnp.int32)
            for j in range(nb):
                beats = (cols[j] > gate) | ((cols[j] == gate) & (lane > j))
                rank = rank + jnp.where(beats, 1, 0)
            row_bias = jnp.where((rank < MOBA_TOPK) & (lane < n), 0.0, NEG_INF)

        def bias_fn(j, s, n=n, row_bias=row_bias):
            s = s - slope * (rel + float((n - j) * blk))
            if row_bias is not None and j < n:
                s = s + row_bias[:, j:j + 1]
            return s

        qb = (q * (MOBA_HEAD_DIM ** -0.5)).astype(BF16)
        o = _attend_tile(n, qb, lambda j: kb_scr[j * blk:(j + 1) * blk, :],
                         lambda j: vb_scr[j * blk:(j + 1) * blk, :], s_scr, bias_fn)
        o_ref[0, n * blk:(n + 1) * blk, :] = o.astype(BF16)


def _moba_attn(qkv, slopes):
    blk = MOBA_BLOCK
    assert blk == TQ_MLA
    spec = lambda which: pl.BlockSpec((1, 1, 1, SEQ, MOBA_HEAD_DIM), lambda b, h: (which, b, h, 0, 0))
    return pl.pallas_call(
        _moba_attn_body,
        grid=(BATCH, MOBA_HEADS),
        in_specs=[pl.BlockSpec((1, 1, 128), lambda b, h: (h, 0, 0)), spec(0), spec(1), spec(2)],
        out_specs=pl.BlockSpec((1, SEQ, MOBA_HEAD_DIM), lambda b, h: (b, 0, h)),
        out_shape=jax.ShapeDtypeStruct((BATCH, SEQ, MOBA_HEADS * MOBA_HEAD_DIM), BF16),
        scratch_shapes=[pltpu.VMEM((SEQ, MOBA_HEAD_DIM), BF16), pltpu.VMEM((SEQ, MOBA_HEAD_DIM), BF16),
                        pltpu.VMEM((blk, N_SCORE_CHUNKS * blk), F32)],
        compiler_params=_params("parallel", "parallel"),
        name="moba_attn",
    )(slopes, qkv, qkv, qkv)


def _merge_body(x_ref, gmix_ref, oa_ref, ob_ref, oc_ref, wga_ref, wgb_ref, wgc_ref, bga_ref, bgb_ref, bgc_ref,
                woa_ref, wob_ref, woc_ref, wout_ref, o_ref, h_scr, acc_scr):
    j = pl.program_id(1)

    @pl.when(j == 0)
    def _():
        h_scr[...] = _rms_bf16(x_ref[...], gmix_ref[...])
        acc_scr[...] = jnp.zeros_like(acc_scr)

    h = h_scr[...]
    merged = None
    for o_i, wg_i, bg_i, wo_i in ((oa_ref, wga_ref, bga_ref, woa_ref), (ob_ref, wgb_ref, bgb_ref, wob_ref),
                                  (oc_ref, wgc_ref, bgc_ref, woc_ref)):
        gate = jax.nn.sigmoid(_dot(h, wg_i[...]) + bg_i[...])
        term = gate * _dot(o_i[...], wo_i[...])
        merged = term if merged is None else merged + term
    acc_scr[...] += _dot(merged.astype(BF16), wout_ref[...])

    @pl.when(j == pl.num_programs(1) - 1)
    def _():
        o_ref[...] = x_ref[...] + acc_scr[...]


def _merge(x, gmix, oa, ob, oc, w_in_b, b_gate, w_oa, w_ob, w_oc, w_out, l):
    tm, tn = TM_MERGE, TN_MERGE
    n_j = D_MODEL // tn
    tok = lambda d: pl.BlockSpec((tm, d), lambda i, j: (i, 0))
    wg = lambda b: pl.BlockSpec((None, D_MODEL, tn), lambda i, j: (l, 0, COL_GATE // tn + b * n_j + j))
    bg = lambda b: pl.BlockSpec((None, 1, tn), lambda i, j: (l, 0, b * n_j + j))
    wo = lambda d: pl.BlockSpec((None, d, tn), lambda i, j: (l, 0, j))
    return pl.pallas_call(
        _merge_body,
        grid=(TOKENS // tm, n_j),
        in_specs=[
            tok(D_MODEL),
            pl.BlockSpec((1, D_MODEL), lambda i, j: (0, 0)),
            tok(MLA_HEADS * MLA_V), tok(CONV_CH), tok(MOBA_HEADS * MOBA_HEAD_DIM),
            wg(0), wg(1), wg(2), bg(0), bg(1), bg(2),
            wo(MLA_HEADS * MLA_V), wo(CONV_CH), wo(MOBA_HEADS * MOBA_HEAD_DIM),
            pl.BlockSpec((None, tn, D_MODEL), lambda i, j: (l, j, 0)),
        ],
        out_specs=tok(D_MODEL),
        out_shape=jax.ShapeDtypeStruct((TOKENS, D_MODEL), F32),
        scratch_shapes=[pltpu.VMEM((tm, D_MODEL), BF16), pltpu.VMEM((tm, D_MODEL), F32)],
        compiler_params=_params("parallel", "arbitrary"),
        name="merge",
    )(x, gmix, oa, ob, oc, w_in_b, w_in_b, w_in_b, b_gate, b_gate, b_gate, w_oa, w_ob, w_oc, w_out)


def _rope_table():
    half = MLA_ROPE // 2
    inv_freq = jnp.exp(-math.log(ROPE_THETA) * jnp.arange(half, dtype=F32) * 2.0 / MLA_ROPE)
    ang = jnp.arange(SEQ, dtype=jnp.int32).astype(F32)[:, None] * inv_freq[None, :]
    cos, sin = jnp.cos(ang), jnp.sin(ang)
    return jnp.concatenate([cos, cos, sin, sin], axis=-1)


def kernel(x, ffn1_norm, ffn1_w_gate, ffn1_w_up, ffn1_w_down, mix_norm, w_in, b_gate, mla_cq_norm, mla_ckv_norm, mla_w_uq, mla_w_ukv, mla_q_norm, mla_k_norm, mla_w_o, conv_w_dw, conv_b_dw, conv_ln_g, conv_ln_b, conv_w_pw, moba_q_norm, moba_k_norm, moba_w_o, w_out, ffn2_norm, ffn2_w_gate, ffn2_w_up, ffn2_w_down):
    cs = _rope_table()
    slopes = jnp.exp2(-8.0 * jnp.arange(1, MOBA_HEADS + 1, dtype=F32) / MOBA_HEADS)
    slopes = jnp.broadcast_to(slopes[:, None, None], (MOBA_HEADS, 1, 128))
    ffn1 = [w.astype(BF16) for w in (ffn1_w_gate, ffn1_w_up, ffn1_w_down)]
    ffn2 = [w.astype(BF16) for w in (ffn2_w_gate, ffn2_w_up, ffn2_w_down)]
    w_in_b = _prep_w_in(w_in)
    wq = mla_w_uq.reshape(DEPTH, MLA_Q_RANK, MLA_HEADS, MLA_QK)
    wq = jnp.concatenate([wq, _rot_cols(wq[..., MLA_NOPE:])], axis=-1).astype(BF16)
    wq = wq.reshape(DEPTH, MLA_Q_RANK, MLA_HEADS * (MLA_NOPE + 2 * MLA_ROPE))
    wkv = mla_w_ukv.astype(BF16)
    w_oa, w_ob, w_oc, w_o = (w.astype(BF16) for w in (mla_w_o, conv_w_pw, moba_w_o, w_out))
    bg = b_gate.reshape(DEPTH, 1, N_BRANCH * D_MODEL)
    xt = x.reshape(TOKENS, D_MODEL)
    for l in range(DEPTH):
        xt = _ffn(xt, ffn1_norm[l].reshape(1, -1), *ffn1, l)
        gmix = mix_norm[l].reshape(1, -1)
        q, k, v = _mla_proj(xt, gmix, w_in_b, l, mla_cq_norm[l], mla_ckv_norm[l], wq, wkv,
                            mla_q_norm[l], mla_k_norm[l], cs)
        o_mla = _mla_attn(q, k, v).reshape(TOKENS, MLA_HEADS * MLA_V)
        z = _glu_proj(xt, gmix, w_in_b, l)
        o_conv = _conv(z, conv_w_dw[l], conv_b_dw[l], conv_ln_g[l], conv_ln_b[l])
        qkv = _moba_proj(xt, gmix, w_in_b, l, moba_q_norm[l], moba_k_norm[l])
        o_moba = _moba_attn(qkv, slopes).reshape(TOKENS, MOBA_HEADS * MOBA_HEAD_DIM)
        xt = _merge(xt, gmix, o_mla, o_conv, o_moba, w_in_b, bg, w_oa, w_ob, w_oc, w_o, l)
        xt = _ffn(xt, ffn2_norm[l].reshape(1, -1), *ffn2, l)
    return xt.reshape(BATCH, SEQ, D_MODEL)
```

```python
import math

import jax
import jax.numpy as jnp
from jax import lax
from jax.experimental import pallas as pl
from jax.experimental.pallas import tpu as pltpu

D_MODEL = 2048
BATCH = 4
SEQ = 2048
DEPTH = 2
TOKENS = BATCH * SEQ

MLA_HEADS = 8
MLA_Q_RANK = 768
MLA_KV_RANK = 512
MLA_NOPE = 128
MLA_ROPE = 64
MLA_QK = MLA_NOPE + MLA_ROPE
MLA_V = 128
ROPE_THETA = 10000.0

CONV_CH = 1024
CONV_WIDTH = 31
CONV_HALO = 32

MOBA_HEADS = 8
MOBA_HEAD_DIM = 128
MOBA_BLOCK = 256
MOBA_TOPK = 3
MOBA_NB = SEQ // MOBA_BLOCK

D_FF = 5632
N_BRANCH = 3
NORM_EPS = 1e-6
NEG_INF = -1e30

OFF_CQ = 0
OFF_CKV = OFF_CQ + MLA_Q_RANK
OFF_KR = OFF_CKV + MLA_KV_RANK
OFF_CONV = OFF_KR + MLA_ROPE
OFF_MOBA = OFF_CONV + 2 * CONV_CH
OFF_GATE = OFF_MOBA + 3 * MOBA_HEADS * MOBA_HEAD_DIM

MLA_LATENT = MLA_Q_RANK + MLA_KV_RANK
MAIN_COLS = N_BRANCH * D_MODEL + OFF_GATE - OFF_CONV
COL_CONV_A = 0
COL_CONV_G = COL_CONV_A + CONV_CH
COL_MOBA = OFF_MOBA - OFF_CONV
COL_GATE = OFF_GATE - OFF_CONV
TR_PREP = 128

F32 = jnp.float32
BF16 = jnp.bfloat16

VMEM_LIMIT_BYTES = 56 * 1024 * 1024

TM_FFN = 512
TF_FFN = 512
TM_PROJ = 512
MLA_PROJ_ROWS = 256
TN_CONV = 512
TM_MERGE = 512
TN_MERGE = 512
TQ_MLA = 256
TS_CONV = 256
CONV_ROWS = 64
CONV_LANES = 256
SUBLANES = 8


def _params(*semantics):
    return pltpu.CompilerParams(dimension_semantics=semantics, vmem_limit_bytes=VMEM_LIMIT_BYTES)


def _rms_bf16(x, g):
    ms = jnp.mean(x * x, axis=-1, keepdims=True)
    return (x * lax.rsqrt(ms + NORM_EPS) * g).astype(BF16)


def _dot(a, b):
    return jnp.dot(a, b, preferred_element_type=F32)


def _dot_nt(a, b):
    return lax.dot_general(a, b, (((1,), (1,)), ((), ())), preferred_element_type=F32)


def _rot_cols(w):
    half = w.shape[-1] // 2
    return jnp.concatenate([-w[..., half:], w[..., :half]], axis=-1)


def _swap_halves(g):
    half = g.shape[-1] // 2
    return jnp.concatenate([g[..., half:], g[..., :half]], axis=-1)


def _prep_w_in_body(w_ref, main_ref, lat_ref, kk_ref):
    main_ref[...] = w_ref[:, OFF_CONV:].astype(BF16)
    lat_ref[...] = w_ref[:, OFF_CQ:OFF_KR].astype(BF16)
    v = w_ref[:, OFF_KR:OFF_KR + 2 * MLA_ROPE]
    lane = lax.broadcasted_iota(jnp.int32, v.shape, 1)
    half = MLA_ROPE // 2
    kk = jnp.where(lane < MLA_ROPE, v,
                   jnp.where(lane < MLA_ROPE + half, -pltpu.roll(v, half, 1), pltpu.roll(v, MLA_ROPE + half, 1)))
    kk_ref[...] = kk.astype(BF16)


def _prep_w_in(w_in):
    d_in = w_in.shape[-1]
    spec = lambda n: pl.BlockSpec((None, TR_PREP, n), lambda l, i: (l, i, 0))
    return pl.pallas_call(
        _prep_w_in_body,
        grid=(DEPTH, D_MODEL // TR_PREP),
        in_specs=[spec(d_in)],
        out_specs=[spec(MAIN_COLS), spec(MLA_LATENT), spec(2 * MLA_ROPE)],
        out_shape=[jax.ShapeDtypeStruct((DEPTH, D_MODEL, MAIN_COLS), BF16),
                   jax.ShapeDtypeStruct((DEPTH, D_MODEL, MLA_LATENT), BF16),
                   jax.ShapeDtypeStruct((DEPTH, D_MODEL, 2 * MLA_ROPE), BF16)],
        compiler_params=_params("parallel", "parallel"),
        name="prep_w_in",
    )(w_in)


def _ffn_body(x_ref, g_ref, wg_ref, wu_ref, wd_ref, o_ref, h_scr, acc_scr):
    j = pl.program_id(1)

    @pl.when(j == 0)
    def _():
        h_scr[...] = _rms_bf16(x_ref[...], g_ref[...])
        acc_scr[...] = jnp.zeros_like(acc_scr)

    h = h_scr[...]
    a = _dot(h, wg_ref[...])
    b = _dot(h, wu_ref[...])
    t = (a * jax.nn.sigmoid(a) * b).astype(BF16)
    acc_scr[...] += _dot(t, wd_ref[...])

    @pl.when(j == pl.num_programs(1) - 1)
    def _():
        o_ref[...] = x_ref[...] + 0.5 * acc_scr[...]


def _ffn(x, g, wg, wu, wd, l):
    tm, tf = TM_FFN, TF_FFN
    return pl.pallas_call(
        _ffn_body,
        grid=(TOKENS // tm, D_FF // tf),
        in_specs=[
            pl.BlockSpec((tm, D_MODEL), lambda i, j: (i, 0)),
            pl.BlockSpec((1, D_MODEL), lambda i, j: (0, 0)),
            pl.BlockSpec((None, D_MODEL, tf), lambda i, j: (l, 0, j)),
            pl.BlockSpec((None, D_MODEL, tf), lambda i, j: (l, 0, j)),
            pl.BlockSpec((None, tf, D_MODEL), lambda i, j: (l, j, 0)),
        ],
        out_specs=pl.BlockSpec((tm, D_MODEL), lambda i, j: (i, 0)),
        out_shape=jax.ShapeDtypeStruct((TOKENS, D_MODEL), F32),
        scratch_shapes=[pltpu.VMEM((tm, D_MODEL), BF16), pltpu.VMEM((tm, D_MODEL), F32)],
        compiler_params=_params("parallel", "arbitrary"),
        name="ffn",
    )(x, g, wg, wu, wd)


def _rope_pair(t, table):
    u = t * table
    return (u + pltpu.roll(u, MLA_ROPE, 1))[:, :MLA_ROPE]


def _mla_proj_body(x_ref, gmix_ref, wc_ref, wkk_ref, gcq_ref, gckv_ref, wq_ref, wkv_ref,
                   gqn_ref, g2q_ref, gkn_ref, g2k_ref, cs_ref, q_ref, k_ref, v_ref):
    for r0 in range(0, x_ref.shape[0], MLA_PROJ_ROWS):
        _mla_proj_rows(slice(r0, r0 + MLA_PROJ_ROWS), x_ref, gmix_ref, wc_ref, wkk_ref, gcq_ref, gckv_ref,
                       wq_ref, wkv_ref, gqn_ref, g2q_ref, gkn_ref, g2k_ref, cs_ref, q_ref, k_ref, v_ref)


def _mla_proj_rows(rows, x_ref, gmix_ref, wc_ref, wkk_ref, gcq_ref, gckv_ref, wq_ref, wkv_ref,
                   gqn_ref, g2q_ref, gkn_ref, g2k_ref, cs_ref, q_ref, k_ref, v_ref):
    hn = _rms_bf16(x_ref[rows, :], gmix_ref[...])
    c = _dot(hn, wc_ref[...])
    cqn = _rms_bf16(c[:, :MLA_Q_RANK], gcq_ref[...])
    ckvn = _rms_bf16(c[:, MLA_Q_RANK:], gckv_ref[...])
    kk = _dot(hn, wkk_ref[...])
    low = lax.broadcasted_iota(jnp.int32, kk.shape, 1) < MLA_ROPE
    cs = cs_ref[rows, :]
    kr_ss = jnp.sum(jnp.where(low, kk * kk, 0.0), axis=-1, keepdims=True)
    kr_rot = _rope_pair(kk, cs * g2k_ref[...])
    cs_q = cs * g2q_ref[...]
    q_scale = MLA_QK ** -0.5
    hw = MLA_NOPE + 2 * MLA_ROPE
    for h in range(MLA_HEADS):
        rq = _dot(cqn, wq_ref[:, h * hw:(h + 1) * hw])
        qn, qt = rq[:, :MLA_NOPE], rq[:, MLA_NOPE:]
        ss = (jnp.sum(qn * qn, axis=-1, keepdims=True)
              + jnp.sum(jnp.where(low, qt * qt, 0.0), axis=-1, keepdims=True)) / MLA_QK
        r = lax.rsqrt(ss + NORM_EPS) * q_scale
        q_ref[0, h, rows, :MLA_NOPE] = (qn * r * gqn_ref[...]).astype(BF16)
        q_ref[0, h, rows, MLA_NOPE:] = (_rope_pair(qt, cs_q) * r).astype(BF16)
        rkv = _dot(ckvn, wkv_ref[:, h * hw:(h + 1) * hw])
        kn = rkv[:, :MLA_NOPE]
        ss = (jnp.sum(kn * kn, axis=-1, keepdims=True) + kr_ss) / MLA_QK
        r = lax.rsqrt(ss + NORM_EPS)
        k_ref[0, h, rows, :MLA_NOPE] = (kn * r * gkn_ref[...]).astype(BF16)
        k_ref[0, h, rows, MLA_NOPE:] = (kr_rot * r).astype(BF16)
        v_ref[0, h, rows, :] = rkv[:, MLA_NOPE:].astype(BF16)


def _mla_proj(x, gmix, w_lat, w_kk, l, gcq, gckv, wq, wkv, gq, gk, cs):
    tm = TM_PROJ
    n_s = SEQ // tm
    gq = gq.reshape(1, MLA_QK)
    gk = gk.reshape(1, MLA_QK)
    pair = lambda g: jnp.concatenate([g[:, MLA_NOPE:], _swap_halves(g[:, MLA_NOPE:])], axis=-1)
    vec = lambda n: pl.BlockSpec((1, n), lambda i: (0, 0))
    head_spec = lambda d: pl.BlockSpec((1, MLA_HEADS, tm, d), lambda i: (i // n_s, 0, i % n_s, 0))
    return pl.pallas_call(
        _mla_proj_body,
        grid=(TOKENS // tm,),
        in_specs=[
            pl.BlockSpec((tm, D_MODEL), lambda i: (i, 0)),
            vec(D_MODEL),
            pl.BlockSpec((None, D_MODEL, MLA_LATENT), lambda i: (l, 0, 0)),
            pl.BlockSpec((None, D_MODEL, 2 * MLA_ROPE), lambda i: (l, 0, 0)),
            vec(MLA_Q_RANK), vec(MLA_KV_RANK),
            pl.BlockSpec((None,) + wq.shape[1:], lambda i: (l, 0, 0)),
            pl.BlockSpec((None,) + wkv.shape[1:], lambda i: (l, 0, 0)),
            vec(MLA_NOPE), vec(2 * MLA_ROPE), vec(MLA_NOPE), vec(2 * MLA_ROPE),
            pl.BlockSpec((tm, 2 * MLA_ROPE), lambda i: (i % n_s, 0)),
        ],
        out_specs=[head_spec(MLA_QK), head_spec(MLA_QK), head_spec(MLA_V)],
        out_shape=[jax.ShapeDtypeStruct((BATCH, MLA_HEADS, SEQ, MLA_QK), BF16),
                   jax.ShapeDtypeStruct((BATCH, MLA_HEADS, SEQ, MLA_QK), BF16),
                   jax.ShapeDtypeStruct((BATCH, MLA_HEADS, SEQ, MLA_V), BF16)],
        compiler_params=_params("parallel"),
        name="mla_proj",
    )(x, gmix, w_lat, w_kk, gcq.reshape(1, -1), gckv.reshape(1, -1), wq, wkv,
      gq[:, :MLA_NOPE], pair(gq), gk[:, :MLA_NOPE], pair(gk), cs)


N_ATTN_TILES = SEQ // TQ_MLA
N_SCORE_CHUNKS = N_ATTN_TILES * (N_ATTN_TILES + 1) // 2


def _attend_tile(qi, q, k_chunk, v_chunk, s_scr, bias_fn):
    t = TQ_MLA
    base = qi * (qi + 1) // 2
    row = lax.broadcasted_iota(jnp.int32, (t, t), 0)
    col = lax.broadcasted_iota(jnp.int32, (t, t), 1)
    mx = None
    for j in range(qi + 1):
        s = _dot_nt(q, k_chunk(j))
        if bias_fn is not None:
            s = bias_fn(j, s)
        if j == qi:
            s = jnp.where(col <= row, s, NEG_INF)
        s_scr[:, (base + j) * t:(base + j + 1) * t] = s
        mx = s if mx is None else jnp.maximum(mx, s)
    m = jnp.max(mx, axis=-1, keepdims=True)
    lsum = None
    acc = None
    for j in range(qi + 1):
        p = jnp.exp(s_scr[:, (base + j) * t:(base + j + 1) * t] - m)
        lsum = p if lsum is None else lsum + p
        pv = _dot(p.astype(BF16), v_chunk(j))
        acc = pv if acc is None else acc + pv
    return acc / jnp.sum(lsum, axis=-1, keepdims=True)


def _mla_attn_body(q_ref, k_ref, v_ref, o_ref, s_scr):
    t = TQ_MLA
    for qi in range(N_ATTN_TILES):
        o = _attend_tile(qi, q_ref[0, 0, qi * t:(qi + 1) * t, :],
                         lambda j: k_ref[0, 0, j * t:(j + 1) * t, :],
                         lambda j: v_ref[0, 0, j * t:(j + 1) * t, :], s_scr, None)
        o_ref[0, qi * t:(qi + 1) * t, :] = o.astype(BF16)


def _mla_attn(q, k, v):
    t = TQ_MLA
    head = lambda d: pl.BlockSpec((1, 1, SEQ, d), lambda b, h: (b, h, 0, 0))
    return pl.pallas_call(
        _mla_attn_body,
        grid=(BATCH, MLA_HEADS),
        in_specs=[head(MLA_QK), head(MLA_QK), head(MLA_V)],
        out_specs=pl.BlockSpec((1, SEQ, MLA_V), lambda b, h: (b, 0, h)),
        out_shape=jax.ShapeDtypeStruct((BATCH, SEQ, MLA_HEADS * MLA_V), BF16),
        scratch_shapes=[pltpu.VMEM((t, N_SCORE_CHUNKS * t), F32)],
        compiler_params=_params("parallel", "parallel"),
        name="mla_attn",
    )(q, k, v)


def _glu_proj_body(x_ref, gmix_ref, wa_ref, wg_ref, z_ref, h_scr):
    @pl.when(pl.program_id(1) == 0)
    def _():
        h_scr[...] = _rms_bf16(x_ref[...], gmix_ref[...])

    h = h_scr[...]
    a = _dot(h, wa_ref[...])
    g = _dot(h, wg_ref[...])
    z_ref[...] = a * jax.nn.sigmoid(g)


def _glu_proj(x, gmix, w_in_b, l):
    tm, tn = TM_PROJ, TN_CONV
    return pl.pallas_call(
        _glu_proj_body,
        grid=(TOKENS // tm, CONV_CH // tn),
        in_specs=[
            pl.BlockSpec((tm, D_MODEL), lambda i, j: (i, 0)),
            pl.BlockSpec((1, D_MODEL), lambda i, j: (0, 0)),
            pl.BlockSpec((None, D_MODEL, tn), lambda i, j: (l, 0, COL_CONV_A // tn + j)),
            pl.BlockSpec((None, D_MODEL, tn), lambda i, j: (l, 0, COL_CONV_G // tn + j)),
        ],
        out_specs=pl.BlockSpec((tm, tn), lambda i, j: (i, j)),
        out_shape=jax.ShapeDtypeStruct((TOKENS, CONV_CH), F32),
        scratch_shapes=[pltpu.VMEM((tm, D_MODEL), BF16)],
        compiler_params=_params("parallel", "arbitrary"),
        name="glu_proj",
    )(x, gmix, w_in_b, w_in_b)


def _conv_body(z_ref, w_ref, b_ref, lg_ref, lb_ref, o_ref, zs):
    ts, halo = TS_CONV, CONV_HALO
    si = pl.program_id(1)

    @pl.when(si == 0)
    def _():
        zs[0, 0:halo, :] = jnp.zeros((halo, CONV_CH), F32)

    @pl.when(si > 0)
    def _():
        zs[0, 0:halo, :] = zs[0, ts:ts + halo, :]

    zs[0, halo:halo + ts, :] = z_ref[...]
    n_shifted = ts + halo - SUBLANES
    for s in range(1, SUBLANES):
        zs[s, 0:n_shifted, :] = zs[0, s:s + n_shifted, :]
    shift = halo - (CONV_WIDTH - 1)
    for r0 in range(0, ts, CONV_ROWS):
        parts = []
        for c0 in range(0, CONV_CH, CONV_LANES):
            acc = jnp.zeros((CONV_ROWS, CONV_LANES), F32) + b_ref[:, c0:c0 + CONV_LANES]
            for k in range(CONV_WIDTH):
                s = (shift + k) % SUBLANES
                a = r0 + shift + k - s
                acc = acc + zs[s, a:a + CONV_ROWS, c0:c0 + CONV_LANES] * w_ref[k:k + 1, c0:c0 + CONV_LANES]
            parts.append(acc)
        acc = jnp.concatenate(parts, axis=-1)
        mu = jnp.mean(acc, axis=-1, keepdims=True)
        xc = acc - mu
        var = jnp.mean(xc * xc, axis=-1, keepdims=True)
        y = xc * lax.rsqrt(var + NORM_EPS) * lg_ref[...] + lb_ref[...]
        o_ref[r0:r0 + CONV_ROWS, :] = (y * jax.nn.sigmoid(y)).astype(BF16)


def _conv(z, w_dw, b_dw, ln_g, ln_b):
    ts = TS_CONV
    n_s = SEQ // ts
    vec = pl.BlockSpec((1, CONV_CH), lambda b, s: (0, 0))
    return pl.pallas_call(
        _conv_body,
        grid=(BATCH, n_s),
        in_specs=[
            pl.BlockSpec((ts, CONV_CH), lambda b, s: (b * n_s + s, 0)),
            pl.BlockSpec((CONV_WIDTH, CONV_CH), lambda b, s: (0, 0)),
            vec, vec, vec,
        ],
        out_specs=pl.BlockSpec((ts, CONV_CH), lambda b, s: (b * n_s + s, 0)),
        out_shape=jax.ShapeDtypeStruct((TOKENS, CONV_CH), BF16),
        scratch_shapes=[pltpu.VMEM((SUBLANES, ts + CONV_HALO, CONV_CH), F32)],
        compiler_params=_params("arbitrary", "arbitrary"),
        name="conv",
    )(z, w_dw, b_dw.reshape(1, -1), ln_g.reshape(1, -1), ln_b.reshape(1, -1))


def _moba_proj_body(x_ref, gmix_ref, w_ref, g_ref, o_ref, h_scr):
    j = pl.program_id(1)

    @pl.when(j == 0)
    def _():
        h_scr[...] = _rms_bf16(x_ref[...], gmix_ref[...])

    u = _dot(h_scr[...], w_ref[...])
    g = g_ref[0]
    for h in range(MOBA_HEADS):
        uh = u[:, h * MOBA_HEAD_DIM:(h + 1) * MOBA_HEAD_DIM]
        ms = jnp.mean(uh * uh, axis=-1, keepdims=True)
        normed = uh * lax.rsqrt(ms + NORM_EPS) * g
        o_ref[0, 0, h] = jnp.where(j < 2, normed, uh)


def _moba_proj(x, gmix, w_in_b, l, gq, gk):
    tm = TM_PROJ
    n_s = SEQ // tm
    hd = MOBA_HEADS * MOBA_HEAD_DIM
    g = jnp.stack([gq, gk, jnp.ones_like(gq)]).reshape(3, 1, MOBA_HEAD_DIM)
    return pl.pallas_call(
        _moba_proj_body,
        grid=(TOKENS // tm, 3),
        in_specs=[
            pl.BlockSpec((tm, D_MODEL), lambda i, j: (i, 0)),
            pl.BlockSpec((1, D_MODEL), lambda i, j: (0, 0)),
            pl.BlockSpec((None, D_MODEL, hd), lambda i, j: (l, 0, COL_MOBA // hd + j)),
            pl.BlockSpec((1, 1, MOBA_HEAD_DIM), lambda i, j: (j, 0, 0)),
        ],
        out_specs=pl.BlockSpec((1, 1, MOBA_HEADS, tm, MOBA_HEAD_DIM),
                               lambda i, j: (j, i // n_s, 0, i % n_s, 0)),
        out_shape=jax.ShapeDtypeStruct((3, BATCH, MOBA_HEADS, SEQ, MOBA_HEAD_DIM), F32),
        scratch_shapes=[pltpu.VMEM((tm, D_MODEL), BF16)],
        compiler_params=_params("parallel", "arbitrary"),
        name="moba_proj",
    )(x, gmix, w_in_b, g)


def _moba_attn_body(slope_ref, q_ref, k_ref, v_ref, o_ref, kb_scr, vb_scr, s_scr):
    blk, nb = MOBA_BLOCK, MOBA_NB
    kb_scr[...] = k_ref[0, 0, 0].astype(BF16)
    vb_scr[...] = v_ref[0, 0, 0].astype(BF16)
    km = jnp.concatenate(
        [jnp.mean(k_ref[0, 0, 0, j * blk:(j + 1) * blk, :], axis=0, keepdims=True) for j in range(nb)], axis=0)
    slope = slope_ref[0][:, 0:1]
    row = lax.broadcasted_iota(jnp.int32, (blk, blk), 0)
    col = lax.broadcasted_iota(jnp.int32, (blk, blk), 1)
    rel = (row - col).astype(F32)
    lane = lax.broadcasted_iota(jnp.int32, (blk, nb), 1)

    for n in range(nb):
        q = q_ref[0, 0, 0, n * blk:(n + 1) * blk, :]
        row_bias = None
        if n > MOBA_TOPK:
            cols = []
            gate = jnp.full((blk, nb), NEG_INF, F32)
            for j in range(nb):
                if j < n:
                    gj = jnp.sum(q * km[j:j + 1, :], axis=-1, keepdims=True)
                    gate = jnp.where(lane == j, gj, gate)
                else:
                    gj = jnp.full((blk, 1), NEG_INF, F32)
                cols.append(gj)
            rank = jnp.zeros((blk, nb), jnp.int32)
            for j in range(nb):
                beats = (cols[j] > gate) | ((cols[j] == gate) & (lane > j))
                rank = rank + jnp.where(beats, 1, 0)
            row_bias = jnp.where((rank < MOBA_TOPK) & (lane < n), 0.0, NEG_INF)

        def bias_fn(j, s, n=n, row_bias=row_bias):
            s = s - slope * (rel + float((n - j) * blk))
            if row_bias is not None and j < n:
                s = s + row_bias[:, j:j + 1]
            return s

        qb = (q * (MOBA_HEAD_DIM ** -0.5)).astype(BF16)
        o = _attend_tile(n, qb, lambda j: kb_scr[j * blk:(j + 1) * blk, :],
                         lambda j: vb_scr[j * blk:(j + 1) * blk, :], s_scr, bias_fn)
        o_ref[0, n * blk:(n + 1) * blk, :] = o.astype(BF16)


def _moba_attn(qkv, slopes):
    blk = MOBA_BLOCK
    assert blk == TQ_MLA
    spec = lambda which: pl.BlockSpec((1, 1, 1, SEQ, MOBA_HEAD_DIM), lambda b, h: (which, b, h, 0, 0))
    return pl.pallas_call(
        _moba_attn_body,
        grid=(BATCH, MOBA_HEADS),
        in_specs=[pl.BlockSpec((1, 1, 128), lambda b, h: (h, 0, 0)), spec(0), spec(1), spec(2)],
        out_specs=pl.BlockSpec((1, SEQ, MOBA_HEAD_DIM), lambda b, h: (b, 0, h)),
        out_shape=jax.ShapeDtypeStruct((BATCH, SEQ, MOBA_HEADS * MOBA_HEAD_DIM), BF16),
        scratch_shapes=[pltpu.VMEM((SEQ, MOBA_HEAD_DIM), BF16), pltpu.VMEM((SEQ, MOBA_HEAD_DIM), BF16),
                        pltpu.VMEM((blk, N_SCORE_CHUNKS * blk), F32)],
        compiler_params=_params("parallel", "parallel"),
        name="moba_attn",
    )(slopes, qkv, qkv, qkv)


def _merge_body(x_ref, gmix_ref, oa_ref, ob_ref, oc_ref, wga_ref, wgb_ref, wgc_ref, bga_ref, bgb_ref, bgc_ref,
                woa_ref, wob_ref, woc_ref, wout_ref, o_ref, h_scr, acc_scr):
    j = pl.program_id(1)

    @pl.when(j == 0)
    def _():
        h_scr[...] = _rms_bf16(x_ref[...], gmix_ref[...])
        acc_scr[...] = jnp.zeros_like(acc_scr)

    h = h_scr[...]
    merged = None
    for o_i, wg_i, bg_i, wo_i in ((oa_ref, wga_ref, bga_ref, woa_ref), (ob_ref, wgb_ref, bgb_ref, wob_ref),
                                  (oc_ref, wgc_ref, bgc_ref, woc_ref)):
        gate = jax.nn.sigmoid(_dot(h, wg_i[...]) + bg_i[...])
        term = gate * _dot(o_i[...], wo_i[...])
        merged = term if merged is None else merged + term
    acc_scr[...] += _dot(merged.astype(BF16), wout_ref[...])

    @pl.when(j == pl.num_programs(1) - 1)
    def _():
        o_ref[...] = x_ref[...] + acc_scr[...]


def _merge(x, gmix, oa, ob, oc, w_in_b, b_gate, w_oa, w_ob, w_oc, w_out, l):
    tm, tn = TM_MERGE, TN_MERGE
    n_j = D_MODEL // tn
    tok = lambda d: pl.BlockSpec((tm, d), lambda i, j: (i, 0))
    wg = lambda b: pl.BlockSpec((None, D_MODEL, tn), lambda i, j: (l, 0, COL_GATE // tn + b * n_j + j))
    bg = lambda b: pl.BlockSpec((None, 1, tn), lambda i, j: (l, 0, b * n_j + j))
    wo = lambda d: pl.BlockSpec((None, d, tn), lambda i, j: (l, 0, j))
    return pl.pallas_call(
        _merge_body,
        grid=(TOKENS // tm, n_j),
        in_specs=[
            tok(D_MODEL),
            pl.BlockSpec((1, D_MODEL), lambda i, j: (0, 0)),
            tok(MLA_HEADS * MLA_V), tok(CONV_CH), tok(MOBA_HEADS * MOBA_HEAD_DIM),
            wg(0), wg(1), wg(2), bg(0), bg(1), bg(2),
            wo(MLA_HEADS * MLA_V), wo(CONV_CH), wo(MOBA_HEADS * MOBA_HEAD_DIM),
            pl.BlockSpec((None, tn, D_MODEL), lambda i, j: (l, j, 0)),
        ],
        out_specs=tok(D_MODEL),
        out_shape=jax.ShapeDtypeStruct((TOKENS, D_MODEL), F32),
        scratch_shapes=[pltpu.VMEM((tm, D_MODEL), BF16), pltpu.VMEM((tm, D_MODEL), F32)],
        compiler_params=_params("parallel", "arbitrary"),
        name="merge",
    )(x, gmix, oa, ob, oc, w_in_b, w_in_b, w_in_b, b_gate, b_gate, b_gate, w_oa, w_ob, w_oc, w_out)


def _rope_table():
    half = MLA_ROPE // 2
    inv_freq = jnp.exp(-math.log(ROPE_THETA) * jnp.arange(half, dtype=F32) * 2.0 / MLA_ROPE)
    ang = jnp.arange(SEQ, dtype=jnp.int32).astype(F32)[:, None] * inv_freq[None, :]
    cos, sin = jnp.cos(ang), jnp.sin(ang)
    return jnp.concatenate([cos, cos, sin, sin], axis=-1)


def kernel(x, ffn1_norm, ffn1_w_gate, ffn1_w_up, ffn1_w_down, mix_norm, w_in, b_gate, mla_cq_norm, mla_ckv_norm, mla_w_uq, mla_w_ukv, mla_q_norm, mla_k_norm, mla_w_o, conv_w_dw, conv_b_dw, conv_ln_g, conv_ln_b, conv_w_pw, moba_q_norm, moba_k_norm, moba_w_o, w_out, ffn2_norm, ffn2_w_gate, ffn2_w_up, ffn2_w_down):
    cs = _rope_table()
    slopes = jnp.exp2(-8.0 * jnp.arange(1, MOBA_HEADS + 1, dtype=F32) / MOBA_HEADS)
    slopes = jnp.broadcast_to(slopes[:, None, None], (MOBA_HEADS, 1, 128))
    ffn1 = [w.astype(BF16) for w in (ffn1_w_gate, ffn1_w_up, ffn1_w_down)]
    ffn2 = [w.astype(BF16) for w in (ffn2_w_gate, ffn2_w_up, ffn2_w_down)]
    w_in_b, w_lat, w_kk = _prep_w_in(w_in)
    wq = mla_w_uq.reshape(DEPTH, MLA_Q_RANK, MLA_HEADS, MLA_QK)
    wq = jnp.concatenate([wq, _rot_cols(wq[..., MLA_NOPE:])], axis=-1).astype(BF16)
    wq = wq.reshape(DEPTH, MLA_Q_RANK, MLA_HEADS * (MLA_NOPE + 2 * MLA_ROPE))
    wkv = mla_w_ukv.astype(BF16)
    w_oa, w_ob, w_oc, w_o = (w.astype(BF16) for w in (mla_w_o, conv_w_pw, moba_w_o, w_out))
    bg = b_gate.reshape(DEPTH, 1, N_BRANCH * D_MODEL)
    xt = x.reshape(TOKENS, D_MODEL)
    for l in range(DEPTH):
        xt = _ffn(xt, ffn1_norm[l].reshape(1, -1), *ffn1, l)
        gmix = mix_norm[l].reshape(1, -1)
        q, k, v = _mla_proj(xt, gmix, w_lat, w_kk, l, mla_cq_norm[l], mla_ckv_norm[l], wq, wkv,
                            mla_q_norm[l], mla_k_norm[l], cs)
        o_mla = _mla_attn(q, k, v).reshape(TOKENS, MLA_HEADS * MLA_V)
        z = _glu_proj(xt, gmix, w_in_b, l)
        o_conv = _conv(z, conv_w_dw[l], conv_b_dw[l], conv_ln_g[l], conv_ln_b[l])
        qkv = _moba_proj(xt, gmix, w_in_b, l, moba_q_norm[l], moba_k_norm[l])
        o_moba = _moba_attn(qkv, slopes).reshape(TOKENS, MOBA_HEADS * MOBA_HEAD_DIM)
        xt = _merge(xt, gmix, o_mla, o_conv, o_moba, w_in_b, bg, w_oa, w_ob, w_oc, w_o, l)
        xt = _ffn(xt, ffn2_norm[l].reshape(1, -1), *ffn2, l)
    return xt.reshape(BATCH, SEQ, D_MODEL)
```

```python
import math

import jax
import jax.numpy as jnp
from jax import lax
from jax.experimental import pallas as pl
from jax.experimental.pallas import tpu as pltpu

D_MODEL = 2048
BATCH = 4
SEQ = 2048
DEPTH = 2
TOKENS = BATCH * SEQ

MLA_HEADS = 8
MLA_Q_RANK = 768
MLA_KV_RANK = 512
MLA_NOPE = 128
MLA_ROPE = 64
MLA_QK = MLA_NOPE + MLA_ROPE
MLA_V = 128
ROPE_THETA = 10000.0

CONV_CH = 1024
CONV_WIDTH = 31
CONV_HALO = 32

MOBA_HEADS = 8
MOBA_HEAD_DIM = 128
MOBA_BLOCK = 256
MOBA_TOPK = 3
MOBA_NB = SEQ // MOBA_BLOCK

D_FF = 5632
N_BRANCH = 3
NORM_EPS = 1e-6
NEG_INF = -1e30
LOG2_E = math.log2(math.e)

OFF_CQ = 0
OFF_CKV = OFF_CQ + MLA_Q_RANK
OFF_KR = OFF_CKV + MLA_KV_RANK
OFF_CONV = OFF_KR + MLA_ROPE
OFF_MOBA = OFF_CONV + 2 * CONV_CH
OFF_GATE = OFF_MOBA + 3 * MOBA_HEADS * MOBA_HEAD_DIM

MLA_LATENT = MLA_Q_RANK + MLA_KV_RANK
MAIN_COLS = N_BRANCH * D_MODEL + OFF_GATE - OFF_CONV
COL_CONV_A = 0
COL_CONV_G = COL_CONV_A + CONV_CH
COL_MOBA = OFF_MOBA - OFF_CONV
COL_GATE = OFF_GATE - OFF_CONV
TR_PREP = 256
PREP_CHUNK = 256

F32 = jnp.float32
BF16 = jnp.bfloat16

VMEM_LIMIT_BYTES = 56 * 1024 * 1024

TM_FFN = 512
TF_FFN = 512
TM_PROJ = 512
MLA_PROJ_ROWS = 256
TN_CONV = 512
TM_MERGE = 512
TN_MERGE = 512
TQ_MLA = 256
TS_CONV = 256
CONV_ROWS = 64
CONV_LANES = 256
SUBLANES = 8


def _params(*semantics):
    return pltpu.CompilerParams(dimension_semantics=semantics, vmem_limit_bytes=VMEM_LIMIT_BYTES)


def _rms_bf16(x, g):
    ms = jnp.mean(x * x, axis=-1, keepdims=True)
    return (x * lax.rsqrt(ms + NORM_EPS) * g).astype(BF16)


def _dot(a, b):
    return jnp.dot(a, b, preferred_element_type=F32)


def _dot_nt(a, b):
    return lax.dot_general(a, b, (((1,), (1,)), ((), ())), preferred_element_type=F32)


def _rot_cols(w):
    half = w.shape[-1] // 2
    return jnp.concatenate([-w[..., half:], w[..., :half]], axis=-1)


def _swap_halves(g):
    half = g.shape[-1] // 2
    return jnp.concatenate([g[..., half:], g[..., :half]], axis=-1)


def _prep_w_in_body(wt_ref, main_ref, lat_ref, kk_ref):
    def put(dst_ref, src0, n_cols):
        for c in range(0, n_cols, PREP_CHUNK):
            dst_ref[:, c:c + PREP_CHUNK] = wt_ref[src0 + c:src0 + c + PREP_CHUNK, :].T.astype(BF16)

    put(main_ref, OFF_CONV, MAIN_COLS)
    put(lat_ref, OFF_CQ, MLA_LATENT)
    v = wt_ref[OFF_KR:OFF_KR + 2 * MLA_ROPE, :].T
    lane = lax.broadcasted_iota(jnp.int32, v.shape, 1)
    half = MLA_ROPE // 2
    kk = jnp.where(lane < MLA_ROPE, v,
                   jnp.where(lane < MLA_ROPE + half, -pltpu.roll(v, half, 1), pltpu.roll(v, MLA_ROPE + half, 1)))
    kk_ref[...] = kk.astype(BF16)


def _prep_w_in(w_in):
    d_in = w_in.shape[-1]
    wt = jnp.swapaxes(w_in, 1, 2)
    spec = lambda n: pl.BlockSpec((None, TR_PREP, n), lambda l, i: (l, i, 0))
    return pl.pallas_call(
        _prep_w_in_body,
        grid=(DEPTH, D_MODEL // TR_PREP),
        in_specs=[pl.BlockSpec((None, d_in, TR_PREP), lambda l, i: (l, 0, i))],
        out_specs=[spec(MAIN_COLS), spec(MLA_LATENT), spec(2 * MLA_ROPE)],
        out_shape=[jax.ShapeDtypeStruct((DEPTH, D_MODEL, MAIN_COLS), BF16),
                   jax.ShapeDtypeStruct((DEPTH, D_MODEL, MLA_LATENT), BF16),
                   jax.ShapeDtypeStruct((DEPTH, D_MODEL, 2 * MLA_ROPE), BF16)],
        compiler_params=_params("parallel", "parallel"),
        name="prep_w_in",
    )(wt)


def _ffn_body(x_ref, g_ref, wg_ref, wu_ref, wd_ref, o_ref, h_scr, acc_scr):
    j = pl.program_id(1)

    @pl.when(j == 0)
    def _():
        h_scr[...] = _rms_bf16(x_ref[...], g_ref[...])
        acc_scr[...] = jnp.zeros_like(acc_scr)

    h = h_scr[...]
    a = _dot(h, wg_ref[...])
    b = _dot(h, wu_ref[...])
    t = (a * jax.nn.sigmoid(a) * b).astype(BF16)
    acc_scr[...] += _dot(t, wd_ref[...])

    @pl.when(j == pl.num_programs(1) - 1)
    def _():
        o_ref[...] = x_ref[...] + 0.5 * acc_scr[...]


def _ffn(x, g, wg, wu, wd, l):
    tm, tf = TM_FFN, TF_FFN
    return pl.pallas_call(
        _ffn_body,
        grid=(TOKENS // tm, D_FF // tf),
        in_specs=[
            pl.BlockSpec((tm, D_MODEL), lambda i, j: (i, 0)),
            pl.BlockSpec((1, D_MODEL), lambda i, j: (0, 0)),
            pl.BlockSpec((None, D_MODEL, tf), lambda i, j: (l, 0, j)),
            pl.BlockSpec((None, D_MODEL, tf), lambda i, j: (l, 0, j)),
            pl.BlockSpec((None, tf, D_MODEL), lambda i, j: (l, j, 0)),
        ],
        out_specs=pl.BlockSpec((tm, D_MODEL), lambda i, j: (i, 0)),
        out_shape=jax.ShapeDtypeStruct((TOKENS, D_MODEL), F32),
        scratch_shapes=[pltpu.VMEM((tm, D_MODEL), BF16), pltpu.VMEM((tm, D_MODEL), F32)],
        compiler_params=_params("parallel", "arbitrary"),
        name="ffn",
    )(x, g, wg, wu, wd)


def _rope_pair(t, table):
    u = t * table
    return (u + pltpu.roll(u, MLA_ROPE, 1))[:, :MLA_ROPE]


def _mla_proj_body(x_ref, gmix_ref, wc_ref, wkk_ref, gcq_ref, gckv_ref, wq_ref, wkv_ref,
                   gqn_ref, g2q_ref, gkn_ref, g2k_ref, cs_ref, q_ref, k_ref, v_ref):
    for r0 in range(0, x_ref.shape[0], MLA_PROJ_ROWS):
        _mla_proj_rows(slice(r0, r0 + MLA_PROJ_ROWS), x_ref, gmix_ref, wc_ref, wkk_ref, gcq_ref, gckv_ref,
                       wq_ref, wkv_ref, gqn_ref, g2q_ref, gkn_ref, g2k_ref, cs_ref, q_ref, k_ref, v_ref)


def _mla_proj_rows(rows, x_ref, gmix_ref, wc_ref, wkk_ref, gcq_ref, gckv_ref, wq_ref, wkv_ref,
                   gqn_ref, g2q_ref, gkn_ref, g2k_ref, cs_ref, q_ref, k_ref, v_ref):
    hn = _rms_bf16(x_ref[rows, :], gmix_ref[...])
    c = _dot(hn, wc_ref[...])
    cqn = _rms_bf16(c[:, :MLA_Q_RANK], gcq_ref[...])
    ckvn = _rms_bf16(c[:, MLA_Q_RANK:], gckv_ref[...])
    kk = _dot(hn, wkk_ref[...])
    low = lax.broadcasted_iota(jnp.int32, kk.shape, 1) < MLA_ROPE
    cs = cs_ref[rows, :]
    kr_ss = jnp.sum(jnp.where(low, kk * kk, 0.0), axis=-1, keepdims=True)
    kr_rot = _rope_pair(kk, cs * g2k_ref[...])
    cs_q = cs * g2q_ref[...]
    q_scale = MLA_QK ** -0.5 * LOG2_E
    hw = MLA_NOPE + 2 * MLA_ROPE
    for h in range(MLA_HEADS):
        rq = _dot(cqn, wq_ref[:, h * hw:(h + 1) * hw])
        qn, qt = rq[:, :MLA_NOPE], rq[:, MLA_NOPE:]
        ss = (jnp.sum(qn * qn, axis=-1, keepdims=True)
              + jnp.sum(jnp.where(low, qt * qt, 0.0), axis=-1, keepdims=True)) / MLA_QK
        r = lax.rsqrt(ss + NORM_EPS) * q_scale
        q_ref[0, h, rows, :MLA_NOPE] = (qn * r * gqn_ref[...]).astype(BF16)
        q_ref[0, h, rows, MLA_NOPE:] = (_rope_pair(qt, cs_q) * r).astype(BF16)
        rkv = _dot(ckvn, wkv_ref[:, h * hw:(h + 1) * hw])
        kn = rkv[:, :MLA_NOPE]
        ss = (jnp.sum(kn * kn, axis=-1, keepdims=True) + kr_ss) / MLA_QK
        r = lax.rsqrt(ss + NORM_EPS)
        k_ref[0, h, rows, :MLA_NOPE] = (kn * r * gkn_ref[...]).astype(BF16)
        k_ref[0, h, rows, MLA_NOPE:] = (kr_rot * r).astype(BF16)
        v_ref[0, h, rows, :] = rkv[:, MLA_NOPE:].astype(BF16)


def _mla_proj(x, gmix, w_lat, w_kk, l, gcq, gckv, wq, wkv, gq, gk, cs):
    tm = TM_PROJ
    n_s = SEQ // tm
    gq = gq.reshape(1, MLA_QK)
    gk = gk.reshape(1, MLA_QK)
    pair = lambda g: jnp.concatenate([g[:, MLA_NOPE:], _swap_halves(g[:, MLA_NOPE:])], axis=-1)
    vec = lambda n: pl.BlockSpec((1, n), lambda i: (0, 0))
    head_spec = lambda d: pl.BlockSpec((1, MLA_HEADS, tm, d), lambda i: (i // n_s, 0, i % n_s, 0))
    return pl.pallas_call(
        _mla_proj_body,
        grid=(TOKENS // tm,),
        in_specs=[
            pl.BlockSpec((tm, D_MODEL), lambda i: (i, 0)),
            vec(D_MODEL),
            pl.BlockSpec((None, D_MODEL, MLA_LATENT), lambda i: (l, 0, 0)),
            pl.BlockSpec((None, D_MODEL, 2 * MLA_ROPE), lambda i: (l, 0, 0)),
            vec(MLA_Q_RANK), vec(MLA_KV_RANK),
            pl.BlockSpec((None,) + wq.shape[1:], lambda i: (l, 0, 0)),
            pl.BlockSpec((None,) + wkv.shape[1:], lambda i: (l, 0, 0)),
            vec(MLA_NOPE), vec(2 * MLA_ROPE), vec(MLA_NOPE), vec(2 * MLA_ROPE),
            pl.BlockSpec((tm, 2 * MLA_ROPE), lambda i: (i % n_s, 0)),
        ],
        out_specs=[head_spec(MLA_QK), head_spec(MLA_QK), head_spec(MLA_V)],
        out_shape=[jax.ShapeDtypeStruct((BATCH, MLA_HEADS, SEQ, MLA_QK), BF16),
                   jax.ShapeDtypeStruct((BATCH, MLA_HEADS, SEQ, MLA_QK), BF16),
                   jax.ShapeDtypeStruct((BATCH, MLA_HEADS, SEQ, MLA_V), BF16)],
        compiler_params=_params("parallel"),
        name="mla_proj",
    )(x, gmix, w_lat, w_kk, gcq.reshape(1, -1), gckv.reshape(1, -1), wq, wkv,
      gq[:, :MLA_NOPE], pair(gq), gk[:, :MLA_NOPE], pair(gk), cs)


N_ATTN_TILES = SEQ // TQ_MLA
N_SCORE_CHUNKS = N_ATTN_TILES * (N_ATTN_TILES + 1) // 2


def _attend_tile(qi, q, k_chunk, v_chunk, s_scr, bias_fn):
    t = TQ_MLA
    base = qi * (qi + 1) // 2
    row = lax.broadcasted_iota(jnp.int32, (t, t), 0)
    col = lax.broadcasted_iota(jnp.int32, (t, t), 1)
    mx = None
    for j in range(qi + 1):
        s = _dot_nt(q, k_chunk(j))
        if bias_fn is not None:
            s = bias_fn(j, s)
        if j == qi:
            s = jnp.where(col <= row, s, NEG_INF)
        s_scr[:, (base + j) * t:(base + j + 1) * t] = s
        mx = s if mx is None else jnp.maximum(mx, s)
    m = jnp.max(mx, axis=-1, keepdims=True)
    lsum = None
    acc = None
    for j in range(qi + 1):
        p = jnp.exp2(s_scr[:, (base + j) * t:(base + j + 1) * t] - m)
        lsum = p if lsum is None else lsum + p
        pv = _dot(p.astype(BF16), v_chunk(j))
        acc = pv if acc is None else acc + pv
    return acc / jnp.sum(lsum, axis=-1, keepdims=True)


def _mla_attn_body(q_ref, k_ref, v_ref, o_ref, s_scr):
    t = TQ_MLA
    for qi in range(N_ATTN_TILES):
        o = _attend_tile(qi, q_ref[0, 0, qi * t:(qi + 1) * t, :],
                         lambda j: k_ref[0, 0, j * t:(j + 1) * t, :],
                         lambda j: v_ref[0, 0, j * t:(j + 1) * t, :], s_scr, None)
        o_ref[0, qi * t:(qi + 1) * t, :] = o.astype(BF16)


def _mla_attn(q, k, v):
    t = TQ_MLA
    head = lambda d: pl.BlockSpec((1, 1, SEQ, d), lambda b, h: (b, h, 0, 0))
    return pl.pallas_call(
        _mla_attn_body,
        grid=(BATCH, MLA_HEADS),
        in_specs=[head(MLA_QK), head(MLA_QK), head(MLA_V)],
        out_specs=pl.BlockSpec((1, SEQ, MLA_V), lambda b, h: (b, 0, h)),
        out_shape=jax.ShapeDtypeStruct((BATCH, SEQ, MLA_HEADS * MLA_V), BF16),
        scratch_shapes=[pltpu.VMEM((t, N_SCORE_CHUNKS * t), F32)],
        compiler_params=_params("parallel", "parallel"),
        name="mla_attn",
    )(q, k, v)


def _glu_proj_body(x_ref, gmix_ref, wa_ref, wg_ref, z_ref, h_scr):
    @pl.when(pl.program_id(1) == 0)
    def _():
        h_scr[...] = _rms_bf16(x_ref[...], gmix_ref[...])

    h = h_scr[...]
    a = _dot(h, wa_ref[...])
    g = _dot(h, wg_ref[...])
    z_ref[...] = a * jax.nn.sigmoid(g)


def _glu_proj(x, gmix, w_in_b, l):
    tm, tn = TM_PROJ, TN_CONV
    return pl.pallas_call(
        _glu_proj_body,
        grid=(TOKENS // tm, CONV_CH // tn),
        in_specs=[
            pl.BlockSpec((tm, D_MODEL), lambda i, j: (i, 0)),
            pl.BlockSpec((1, D_MODEL), lambda i, j: (0, 0)),
            pl.BlockSpec((None, D_MODEL, tn), lambda i, j: (l, 0, COL_CONV_A // tn + j)),
            pl.BlockSpec((None, D_MODEL, tn), lambda i, j: (l, 0, COL_CONV_G // tn + j)),
        ],
        out_specs=pl.BlockSpec((tm, tn), lambda i, j: (i, j)),
        out_shape=jax.ShapeDtypeStruct((TOKENS, CONV_CH), F32),
        scratch_shapes=[pltpu.VMEM((tm, D_MODEL), BF16)],
        compiler_params=_params("parallel", "arbitrary"),
        name="glu_proj",
    )(x, gmix, w_in_b, w_in_b)


def _conv_body(z_ref, w_ref, b_ref, lg_ref, lb_ref, o_ref, zs):
    ts, halo = TS_CONV, CONV_HALO
    si = pl.program_id(1)

    @pl.when(si == 0)
    def _():
        zs[0, 0:halo, :] = jnp.zeros((halo, CONV_CH), F32)

    @pl.when(si > 0)
    def _():
        zs[0, 0:halo, :] = zs[0, ts:ts + halo, :]

    zs[0, halo:halo + ts, :] = z_ref[...]
    n_shifted = ts + halo - SUBLANES
    for s in range(1, SUBLANES):
        zs[s, 0:n_shifted, :] = zs[0, s:s + n_shifted, :]
    shift = halo - (CONV_WIDTH - 1)
    for r0 in range(0, ts, CONV_ROWS):
        parts = []
        for c0 in range(0, CONV_CH, CONV_LANES):
            acc = jnp.zeros((CONV_ROWS, CONV_LANES), F32) + b_ref[:, c0:c0 + CONV_LANES]
            for k in range(CONV_WIDTH):
                s = (shift + k) % SUBLANES
                a = r0 + shift + k - s
                acc = acc + zs[s, a:a + CONV_ROWS, c0:c0 + CONV_LANES] * w_ref[k:k + 1, c0:c0 + CONV_LANES]
            parts.append(acc)
        acc = jnp.concatenate(parts, axis=-1)
        mu = jnp.mean(acc, axis=-1, keepdims=True)
        xc = acc - mu
        var = jnp.mean(xc * xc, axis=-1, keepdims=True)
        y = xc * lax.rsqrt(var + NORM_EPS) * lg_ref[...] + lb_ref[...]
        o_ref[r0:r0 + CONV_ROWS, :] = (y * jax.nn.sigmoid(y)).astype(BF16)


def _conv(z, w_dw, b_dw, ln_g, ln_b):
    ts = TS_CONV
    n_s = SEQ // ts
    vec = pl.BlockSpec((1, CONV_CH), lambda b, s: (0, 0))
    return pl.pallas_call(
        _conv_body,
        grid=(BATCH, n_s),
        in_specs=[
            pl.BlockSpec((ts, CONV_CH), lambda b, s: (b * n_s + s, 0)),
            pl.BlockSpec((CONV_WIDTH, CONV_CH), lambda b, s: (0, 0)),
            vec, vec, vec,
        ],
        out_specs=pl.BlockSpec((ts, CONV_CH), lambda b, s: (b * n_s + s, 0)),
        out_shape=jax.ShapeDtypeStruct((TOKENS, CONV_CH), BF16),
        scratch_shapes=[pltpu.VMEM((SUBLANES, ts + CONV_HALO, CONV_CH), F32)],
        compiler_params=_params("arbitrary", "arbitrary"),
        name="conv",
    )(z, w_dw, b_dw.reshape(1, -1), ln_g.reshape(1, -1), ln_b.reshape(1, -1))


def _moba_proj_body(x_ref, gmix_ref, w_ref, g_ref, o_ref, h_scr):
    j = pl.program_id(1)

    @pl.when(j == 0)
    def _():
        h_scr[...] = _rms_bf16(x_ref[...], gmix_ref[...])

    u = _dot(h_scr[...], w_ref[...])
    g = g_ref[0]
    for h in range(MOBA_HEADS):
        uh = u[:, h * MOBA_HEAD_DIM:(h + 1) * MOBA_HEAD_DIM]
        ms = jnp.mean(uh * uh, axis=-1, keepdims=True)
        normed = uh * lax.rsqrt(ms + NORM_EPS) * g
        o_ref[0, 0, h] = jnp.where(j < 2, normed, uh)


def _moba_proj(x, gmix, w_in_b, l, gq, gk):
    tm = TM_PROJ
    n_s = SEQ // tm
    hd = MOBA_HEADS * MOBA_HEAD_DIM
    g = jnp.stack([gq, gk, jnp.ones_like(gq)]).reshape(3, 1, MOBA_HEAD_DIM)
    return pl.pallas_call(
        _moba_proj_body,
        grid=(TOKENS // tm, 3),
        in_specs=[
            pl.BlockSpec((tm, D_MODEL), lambda i, j: (i, 0)),
            pl.BlockSpec((1, D_MODEL), lambda i, j: (0, 0)),
            pl.BlockSpec((None, D_MODEL, hd), lambda i, j: (l, 0, COL_MOBA // hd + j)),
            pl.BlockSpec((1, 1, MOBA_HEAD_DIM), lambda i, j: (j, 0, 0)),
        ],
        out_specs=pl.BlockSpec((1, 1, MOBA_HEADS, tm, MOBA_HEAD_DIM),
                               lambda i, j: (j, i // n_s, 0, i % n_s, 0)),
        out_shape=jax.ShapeDtypeStruct((3, BATCH, MOBA_HEADS, SEQ, MOBA_HEAD_DIM), F32),
        scratch_shapes=[pltpu.VMEM((tm, D_MODEL), BF16)],
        compiler_params=_params("parallel", "arbitrary"),
        name="moba_proj",
    )(x, gmix, w_in_b, g)


def _moba_attn_body(slope_ref, q_ref, k_ref, v_ref, o_ref, kb_scr, vb_scr, s_scr):
    blk, nb = MOBA_BLOCK, MOBA_NB
    kb_scr[...] = k_ref[0, 0, 0].astype(BF16)
    vb_scr[...] = v_ref[0, 0, 0].astype(BF16)
    km = jnp.concatenate(
        [jnp.mean(k_ref[0, 0, 0, j * blk:(j + 1) * blk, :], axis=0, keepdims=True) for j in range(nb)], axis=0)
    slope = slope_ref[0][:, 0:1] * LOG2_E
    key_pos = lax.broadcasted_iota(jnp.int32, (1, blk), 1).astype(F32)
    lane = lax.broadcasted_iota(jnp.int32, (blk, nb), 1)

    for n in range(nb):
        q = q_ref[0, 0, 0, n * blk:(n + 1) * blk, :]
        row_bias = None
        if n > MOBA_TOPK:
            cols = []
            gate = jnp.full((blk, nb), NEG_INF, F32)
            for j in range(nb):
                if j < n:
                    gj = jnp.sum(q * km[j:j + 1, :], axis=-1, keepdims=True)
                    gate = jnp.where(lane == j, gj, gate)
                else:
                    gj = jnp.full((blk, 1), NEG_INF, F32)
                cols.append(gj)
            rank = jnp.zeros((blk, nb), jnp.int32)
            for j in range(nb):
                beats = (cols[j] > gate) | ((cols[j] == gate) & (lane > j))
                rank = rank + jnp.where(beats, 1, 0)
            row_bias = jnp.where((rank < MOBA_TOPK) & (lane < n), 0.0, NEG_INF)

        def bias_fn(j, s, n=n, row_bias=row_bias):
            s = s + slope * (key_pos + float(j * blk))
            if row_bias is not None and j < n:
                s = s + row_bias[:, j:j + 1]
            return s

        qb = (q * (MOBA_HEAD_DIM ** -0.5 * LOG2_E)).astype(BF16)
        o = _attend_tile(n, qb, lambda j: kb_scr[j * blk:(j + 1) * blk, :],
                         lambda j: vb_scr[j * blk:(j + 1) * blk, :], s_scr, bias_fn)
        o_ref[0, n * blk:(n + 1) * blk, :] = o.astype(BF16)


def _moba_attn(qkv, slopes):
    blk = MOBA_BLOCK
    assert blk == TQ_MLA
    spec = lambda which: pl.BlockSpec((1, 1, 1, SEQ, MOBA_HEAD_DIM), lambda b, h: (which, b, h, 0, 0))
    return pl.pallas_call(
        _moba_attn_body,
        grid=(BATCH, MOBA_HEADS),
        in_specs=[pl.BlockSpec((1, 1, 128), lambda b, h: (h, 0, 0)), spec(0), spec(1), spec(2)],
        out_specs=pl.BlockSpec((1, SEQ, MOBA_HEAD_DIM), lambda b, h: (b, 0, h)),
        out_shape=jax.ShapeDtypeStruct((BATCH, SEQ, MOBA_HEADS * MOBA_HEAD_DIM), BF16),
        scratch_shapes=[pltpu.VMEM((SEQ, MOBA_HEAD_DIM), BF16), pltpu.VMEM((SEQ, MOBA_HEAD_DIM), BF16),
                        pltpu.VMEM((blk, N_SCORE_CHUNKS * blk), F32)],
        compiler_params=_params("parallel", "parallel"),
        name="moba_attn",
    )(slopes, qkv, qkv, qkv)


def _merge_body(x_ref, gmix_ref, oa_ref, ob_ref, oc_ref, wga_ref, wgb_ref, wgc_ref, bga_ref, bgb_ref, bgc_ref,
                woa_ref, wob_ref, woc_ref, wout_ref, o_ref, h_scr, acc_scr):
    j = pl.program_id(1)

    @pl.when(j == 0)
    def _():
        h_scr[...] = _rms_bf16(x_ref[...], gmix_ref[...])
        acc_scr[...] = jnp.zeros_like(acc_scr)

    h = h_scr[...]
    merged = None
    for o_i, wg_i, bg_i, wo_i in ((oa_ref, wga_ref, bga_ref, woa_ref), (ob_ref, wgb_ref, bgb_ref, wob_ref),
                                  (oc_ref, wgc_ref, bgc_ref, woc_ref)):
        gate = jax.nn.sigmoid(_dot(h, wg_i[...]) + bg_i[...])
        term = gate * _dot(o_i[...], wo_i[...])
        merged = term if merged is None else merged + term
    acc_scr[...] += _dot(merged.astype(BF16), wout_ref[...])

    @pl.when(j == pl.num_programs(1) - 1)
    def _():
        o_ref[...] = x_ref[...] + acc_scr[...]


def _merge(x, gmix, oa, ob, oc, w_in_b, b_gate, w_oa, w_ob, w_oc, w_out, l):
    tm, tn = TM_MERGE, TN_MERGE
    n_j = D_MODEL // tn
    tok = lambda d: pl.BlockSpec((tm, d), lambda i, j: (i, 0))
    wg = lambda b: pl.BlockSpec((None, D_MODEL, tn), lambda i, j: (l, 0, COL_GATE // tn + b * n_j + j))
    bg = lambda b: pl.BlockSpec((None, 1, tn), lambda i, j: (l, 0, b * n_j + j))
    wo = lambda d: pl.BlockSpec((None, d, tn), lambda i, j: (l, 0, j))
    return pl.pallas_call(
        _merge_body,
        grid=(TOKENS // tm, n_j),
        in_specs=[
            tok(D_MODEL),
            pl.BlockSpec((1, D_MODEL), lambda i, j: (0, 0)),
            tok(MLA_HEADS * MLA_V), tok(CONV_CH), tok(MOBA_HEADS * MOBA_HEAD_DIM),
            wg(0), wg(1), wg(2), bg(0), bg(1), bg(2),
            wo(MLA_HEADS * MLA_V), wo(CONV_CH), wo(MOBA_HEADS * MOBA_HEAD_DIM),
            pl.BlockSpec((None, tn, D_MODEL), lambda i, j: (l, j, 0)),
        ],
        out_specs=tok(D_MODEL),
        out_shape=jax.ShapeDtypeStruct((TOKENS, D_MODEL), F32),
        scratch_shapes=[pltpu.VMEM((tm, D_MODEL), BF16), pltpu.VMEM((tm, D_MODEL), F32)],
        compiler_params=_params("parallel", "arbitrary"),
        name="merge",
    )(x, gmix, oa, ob, oc, w_in_b, w_in_b, w_in_b, b_gate, b_gate, b_gate, w_oa, w_ob, w_oc, w_out)


def _rope_table():
    half = MLA_ROPE // 2
    inv_freq = jnp.exp(-math.log(ROPE_THETA) * jnp.arange(half, dtype=F32) * 2.0 / MLA_ROPE)
    ang = jnp.arange(SEQ, dtype=jnp.int32).astype(F32)[:, None] * inv_freq[None, :]
    cos, sin = jnp.cos(ang), jnp.sin(ang)
    return jnp.concatenate([cos, cos, sin, sin], axis=-1)


def kernel(x, ffn1_norm, ffn1_w_gate, ffn1_w_up, ffn1_w_down, mix_norm, w_in, b_gate, mla_cq_norm, mla_ckv_norm, mla_w_uq, mla_w_ukv, mla_q_norm, mla_k_norm, mla_w_o, conv_w_dw, conv_b_dw, conv_ln_g, conv_ln_b, conv_w_pw, moba_q_norm, moba_k_norm, moba_w_o, w_out, ffn2_norm, ffn2_w_gate, ffn2_w_up, ffn2_w_down):
    cs = _rope_table()
    slopes = jnp.exp2(-8.0 * jnp.arange(1, MOBA_HEADS + 1, dtype=F32) / MOBA_HEADS)
    slopes = jnp.broadcast_to(slopes[:, None, None], (MOBA_HEADS, 1, 128))
    ffn1 = [w.astype(BF16) for w in (ffn1_w_gate, ffn1_w_up, ffn1_w_down)]
    ffn2 = [w.astype(BF16) for w in (ffn2_w_gate, ffn2_w_up, ffn2_w_down)]
    w_in_b, w_lat, w_kk = _prep_w_in(w_in)
    wq = mla_w_uq.reshape(DEPTH, MLA_Q_RANK, MLA_HEADS, MLA_QK)
    wq = jnp.concatenate([wq, _rot_cols(wq[..., MLA_NOPE:])], axis=-1).astype(BF16)
    wq = wq.reshape(DEPTH, MLA_Q_RANK, MLA_HEADS * (MLA_NOPE + 2 * MLA_ROPE))
    wkv = mla_w_ukv.astype(BF16)
    w_oa, w_ob, w_oc, w_o = (w.astype(BF16) for w in (mla_w_o, conv_w_pw, moba_w_o, w_out))
    bg = b_gate.reshape(DEPTH, 1, N_BRANCH * D_MODEL)
    xt = x.reshape(TOKENS, D_MODEL)
    for l in range(DEPTH):
        xt = _ffn(xt, ffn1_norm[l].reshape(1, -1), *ffn1, l)
        gmix = mix_norm[l].reshape(1, -1)
        q, k, v = _mla_proj(xt, gmix, w_lat, w_kk, l, mla_cq_norm[l], mla_ckv_norm[l], wq, wkv,
                            mla_q_norm[l], mla_k_norm[l], cs)
        o_mla = _mla_attn(q, k, v).reshape(TOKENS, MLA_HEADS * MLA_V)
        z = _glu_proj(xt, gmix, w_in_b, l)
        o_conv = _conv(z, conv_w_dw[l], conv_b_dw[l], conv_ln_g[l], conv_ln_b[l])
        qkv = _moba_proj(xt, gmix, w_in_b, l, moba_q_norm[l], moba_k_norm[l])
        o_moba = _moba_attn(qkv, slopes).reshape(TOKENS, MOBA_HEADS * MOBA_HEAD_DIM)
        xt = _merge(xt, gmix, o_mla, o_conv, o_moba, w_in_b, bg, w_oa, w_ob, w_oc, w_o, l)
        xt = _ffn(xt, ffn2_norm[l].reshape(1, -1), *ffn2, l)
    return xt.reshape(BATCH, SEQ, D_MODEL)
```

```python
import math

import jax
import jax.numpy as jnp
from jax import lax
from jax.experimental import pallas as pl
from jax.experimental.pallas import tpu as pltpu

D_MODEL = 2048
BATCH = 4
SEQ = 2048
DEPTH = 2
TOKENS = BATCH * SEQ

MLA_HEADS = 8
MLA_Q_RANK = 768
MLA_KV_RANK = 512
MLA_NOPE = 128
MLA_ROPE = 64
MLA_QK = MLA_NOPE + MLA_ROPE
MLA_V = 128
ROPE_THETA = 10000.0

CONV_CH = 1024
CONV_WIDTH = 31
CONV_HALO = 32

MOBA_HEADS = 8
MOBA_HEAD_DIM = 128
MOBA_BLOCK = 256
MOBA_TOPK = 3
MOBA_NB = SEQ // MOBA_BLOCK

D_FF = 5632
N_BRANCH = 3
NORM_EPS = 1e-6
NEG_INF = -1e30
LOG2_E = math.log2(math.e)

OFF_CQ = 0
OFF_CKV = OFF_CQ + MLA_Q_RANK
OFF_KR = OFF_CKV + MLA_KV_RANK
OFF_CONV = OFF_KR + MLA_ROPE
OFF_MOBA = OFF_CONV + 2 * CONV_CH
OFF_GATE = OFF_MOBA + 3 * MOBA_HEADS * MOBA_HEAD_DIM

MLA_LATENT = MLA_Q_RANK + MLA_KV_RANK
MAIN_COLS = N_BRANCH * D_MODEL + OFF_GATE - OFF_CONV
COL_CONV_A = 0
COL_CONV_G = COL_CONV_A + CONV_CH
COL_MOBA = OFF_MOBA - OFF_CONV
COL_GATE = OFF_GATE - OFF_CONV
TR_PREP = 256
PREP_CHUNK = 256

F32 = jnp.float32
BF16 = jnp.bfloat16

VMEM_LIMIT_BYTES = 56 * 1024 * 1024

TM_FFN = 512
TF_FFN = 512
TM_PROJ = 512
MLA_PROJ_ROWS = 128
TN_CONV = 512
TM_MERGE = 512
TN_MERGE = 512
TQ_MLA = 256
TS_CONV = 256
CONV_ROWS = 64
CONV_LANES = 256
SUBLANES = 8


def _params(*semantics):
    return pltpu.CompilerParams(dimension_semantics=semantics, vmem_limit_bytes=VMEM_LIMIT_BYTES)


def _rms_bf16(x, g):
    ms = jnp.mean(x * x, axis=-1, keepdims=True)
    return (x * lax.rsqrt(ms + NORM_EPS) * g).astype(BF16)


def _dot(a, b):
    return jnp.dot(a, b, preferred_element_type=F32)


def _dot_nt(a, b):
    return lax.dot_general(a, b, (((1,), (1,)), ((), ())), preferred_element_type=F32)


def _rot_cols(w):
    half = w.shape[-1] // 2
    return jnp.concatenate([-w[..., half:], w[..., :half]], axis=-1)


def _swap_halves(g):
    half = g.shape[-1] // 2
    return jnp.concatenate([g[..., half:], g[..., :half]], axis=-1)


def _prep_w_in_body(wt_ref, main_ref, lat_ref, kk_ref):
    def put(dst_ref, src0, n_cols):
        for c in range(0, n_cols, PREP_CHUNK):
            dst_ref[:, c:c + PREP_CHUNK] = wt_ref[src0 + c:src0 + c + PREP_CHUNK, :].T.astype(BF16)

    put(main_ref, OFF_CONV, MAIN_COLS)
    put(lat_ref, OFF_CQ, MLA_LATENT)
    v = wt_ref[OFF_KR:OFF_KR + 2 * MLA_ROPE, :].T
    lane = lax.broadcasted_iota(jnp.int32, v.shape, 1)
    half = MLA_ROPE // 2
    kk = jnp.where(lane < MLA_ROPE, v,
                   jnp.where(lane < MLA_ROPE + half, -pltpu.roll(v, half, 1), pltpu.roll(v, MLA_ROPE + half, 1)))
    kk_ref[...] = kk.astype(BF16)


def _prep_w_in(w_in):
    d_in = w_in.shape[-1]
    wt = jnp.swapaxes(w_in, 1, 2)
    spec = lambda n: pl.BlockSpec((None, TR_PREP, n), lambda l, i: (l, i, 0))
    return pl.pallas_call(
        _prep_w_in_body,
        grid=(DEPTH, D_MODEL // TR_PREP),
        in_specs=[pl.BlockSpec((None, d_in, TR_PREP), lambda l, i: (l, 0, i))],
        out_specs=[spec(MAIN_COLS), spec(MLA_LATENT), spec(2 * MLA_ROPE)],
        out_shape=[jax.ShapeDtypeStruct((DEPTH, D_MODEL, MAIN_COLS), BF16),
                   jax.ShapeDtypeStruct((DEPTH, D_MODEL, MLA_LATENT), BF16),
                   jax.ShapeDtypeStruct((DEPTH, D_MODEL, 2 * MLA_ROPE), BF16)],
        compiler_params=_params("parallel", "parallel"),
        name="prep_w_in",
    )(wt)


def _ffn_body(x_ref, g_ref, wg_ref, wu_ref, wd_ref, o_ref, h_scr, acc_scr):
    j = pl.program_id(1)

    @pl.when(j == 0)
    def _():
        h_scr[...] = _rms_bf16(x_ref[...], g_ref[...])
        acc_scr[...] = jnp.zeros_like(acc_scr)

    h = h_scr[...]
    a = _dot(h, wg_ref[...])
    b = _dot(h, wu_ref[...])
    t = (a * jax.nn.sigmoid(a) * b).astype(BF16)
    acc_scr[...] += _dot(t, wd_ref[...])

    @pl.when(j == pl.num_programs(1) - 1)
    def _():
        o_ref[...] = x_ref[...] + 0.5 * acc_scr[...]


def _ffn(x, g, wg, wu, wd, l):
    tm, tf = TM_FFN, TF_FFN
    return pl.pallas_call(
        _ffn_body,
        grid=(TOKENS // tm, D_FF // tf),
        in_specs=[
            pl.BlockSpec((tm, D_MODEL), lambda i, j: (i, 0)),
            pl.BlockSpec((1, D_MODEL), lambda i, j: (0, 0)),
            pl.BlockSpec((None, D_MODEL, tf), lambda i, j: (l, 0, j)),
            pl.BlockSpec((None, D_MODEL, tf), lambda i, j: (l, 0, j)),
            pl.BlockSpec((None, tf, D_MODEL), lambda i, j: (l, j, 0)),
        ],
        out_specs=pl.BlockSpec((tm, D_MODEL), lambda i, j: (i, 0)),
        out_shape=jax.ShapeDtypeStruct((TOKENS, D_MODEL), F32),
        scratch_shapes=[pltpu.VMEM((tm, D_MODEL), BF16), pltpu.VMEM((tm, D_MODEL), F32)],
        compiler_params=_params("parallel", "arbitrary"),
        name="ffn",
    )(x, g, wg, wu, wd)


def _rope_pair(t, table):
    u = t * table
    return (u + pltpu.roll(u, MLA_ROPE, 1))[:, :MLA_ROPE]


def _mla_proj_body(x_ref, gmix_ref, wc_ref, wkk_ref, gcq_ref, gckv_ref, wq_ref, wkv_ref,
                   gqn_ref, g2q_ref, gkn_ref, g2k_ref, cs_ref, q_ref, k_ref, v_ref):
    for r0 in range(0, x_ref.shape[0], MLA_PROJ_ROWS):
        _mla_proj_rows(slice(r0, r0 + MLA_PROJ_ROWS), x_ref, gmix_ref, wc_ref, wkk_ref, gcq_ref, gckv_ref,
                       wq_ref, wkv_ref, gqn_ref, g2q_ref, gkn_ref, g2k_ref, cs_ref, q_ref, k_ref, v_ref)


def _mla_proj_rows(rows, x_ref, gmix_ref, wc_ref, wkk_ref, gcq_ref, gckv_ref, wq_ref, wkv_ref,
                   gqn_ref, g2q_ref, gkn_ref, g2k_ref, cs_ref, q_ref, k_ref, v_ref):
    hn = _rms_bf16(x_ref[rows, :], gmix_ref[...])
    c = _dot(hn, wc_ref[...])
    cqn = _rms_bf16(c[:, :MLA_Q_RANK], gcq_ref[...])
    ckvn = _rms_bf16(c[:, MLA_Q_RANK:], gckv_ref[...])
    kk = _dot(hn, wkk_ref[...])
    low = lax.broadcasted_iota(jnp.int32, kk.shape, 1) < MLA_ROPE
    cs = cs_ref[rows, :]
    kr_ss = jnp.sum(jnp.where(low, kk * kk, 0.0), axis=-1, keepdims=True)
    kr_rot = _rope_pair(kk, cs * g2k_ref[...])
    cs_q = cs * g2q_ref[...]
    q_scale = MLA_QK ** -0.5 * LOG2_E
    hw = MLA_NOPE + 2 * MLA_ROPE
    for h in range(MLA_HEADS):
        rq = _dot(cqn, wq_ref[:, h * hw:(h + 1) * hw])
        qn, qt = rq[:, :MLA_NOPE], rq[:, MLA_NOPE:]
        ss = (jnp.sum(qn * qn, axis=-1, keepdims=True)
              + jnp.sum(jnp.where(low, qt * qt, 0.0), axis=-1, keepdims=True)) / MLA_QK
        r = lax.rsqrt(ss + NORM_EPS) * q_scale
        q_ref[0, h, rows, :MLA_NOPE] = (qn * r * gqn_ref[...]).astype(BF16)
        q_ref[0, h, rows, MLA_NOPE:] = (_rope_pair(qt, cs_q) * r).astype(BF16)
        rkv = _dot(ckvn, wkv_ref[:, h * hw:(h + 1) * hw])
        kn = rkv[:, :MLA_NOPE]
        ss = (jnp.sum(kn * kn, axis=-1, keepdims=True) + kr_ss) / MLA_QK
        r = lax.rsqrt(ss + NORM_EPS)
        k_ref[0, h, rows, :MLA_NOPE] = (kn * r * gkn_ref[...]).astype(BF16)
        k_ref[0, h, rows, MLA_NOPE:] = (kr_rot * r).astype(BF16)
        v_ref[0, h, rows, :] = rkv[:, MLA_NOPE:].astype(BF16)


def _mla_proj(x, gmix, w_lat, w_kk, l, gcq, gckv, wq, wkv, gq, gk, cs):
    tm = TM_PROJ
    n_s = SEQ // tm
    gq = gq.reshape(1, MLA_QK)
    gk = gk.reshape(1, MLA_QK)
    pair = lambda g: jnp.concatenate([g[:, MLA_NOPE:], _swap_halves(g[:, MLA_NOPE:])], axis=-1)
    vec = lambda n: pl.BlockSpec((1, n), lambda i: (0, 0))
    head_spec = lambda d: pl.BlockSpec((1, MLA_HEADS, tm, d), lambda i: (i // n_s, 0, i % n_s, 0))
    return pl.pallas_call(
        _mla_proj_body,
        grid=(TOKENS // tm,),
        in_specs=[
            pl.BlockSpec((tm, D_MODEL), lambda i: (i, 0)),
            vec(D_MODEL),
            pl.BlockSpec((None, D_MODEL, MLA_LATENT), lambda i: (l, 0, 0)),
            pl.BlockSpec((None, D_MODEL, 2 * MLA_ROPE), lambda i: (l, 0, 0)),
            vec(MLA_Q_RANK), vec(MLA_KV_RANK),
            pl.BlockSpec((None,) + wq.shape[1:], lambda i: (l, 0, 0)),
            pl.BlockSpec((None,) + wkv.shape[1:], lambda i: (l, 0, 0)),
            vec(MLA_NOPE), vec(2 * MLA_ROPE), vec(MLA_NOPE), vec(2 * MLA_ROPE),
            pl.BlockSpec((tm, 2 * MLA_ROPE), lambda i: (i % n_s, 0)),
        ],
        out_specs=[head_spec(MLA_QK), head_spec(MLA_QK), head_spec(MLA_V)],
        out_shape=[jax.ShapeDtypeStruct((BATCH, MLA_HEADS, SEQ, MLA_QK), BF16),
                   jax.ShapeDtypeStruct((BATCH, MLA_HEADS, SEQ, MLA_QK), BF16),
                   jax.ShapeDtypeStruct((BATCH, MLA_HEADS, SEQ, MLA_V), BF16)],
        compiler_params=_params("parallel"),
        name="mla_proj",
    )(x, gmix, w_lat, w_kk, gcq.reshape(1, -1), gckv.reshape(1, -1), wq, wkv,
      gq[:, :MLA_NOPE], pair(gq), gk[:, :MLA_NOPE], pair(gk), cs)


N_ATTN_TILES = SEQ // TQ_MLA
N_SCORE_CHUNKS = N_ATTN_TILES * (N_ATTN_TILES + 1) // 2


def _attend_tile(qi, q, k_chunk, v_chunk, s_scr, bias_fn):
    t = TQ_MLA
    base = qi * (qi + 1) // 2
    row = lax.broadcasted_iota(jnp.int32, (t, t), 0)
    col = lax.broadcasted_iota(jnp.int32, (t, t), 1)
    mx = None
    for j in range(qi + 1):
        s = _dot_nt(q, k_chunk(j))
        if bias_fn is not None:
            s = bias_fn(j, s)
        if j == qi:
            s = jnp.where(col <= row, s, NEG_INF)
        s_scr[:, (base + j) * t:(base + j + 1) * t] = s
        mx = s if mx is None else jnp.maximum(mx, s)
    m = jnp.max(mx, axis=-1, keepdims=True)
    lsum = None
    acc = None
    for j in range(qi + 1):
        p = jnp.exp2(s_scr[:, (base + j) * t:(base + j + 1) * t] - m)
        lsum = p if lsum is None else lsum + p
        pv = _dot(p.astype(BF16), v_chunk(j))
        acc = pv if acc is None else acc + pv
    return acc / jnp.sum(lsum, axis=-1, keepdims=True)


def _mla_attn_body(q_ref, k_ref, v_ref, o_ref, s_scr):
    t = TQ_MLA
    for qi in range(N_ATTN_TILES):
        o = _attend_tile(qi, q_ref[0, 0, qi * t:(qi + 1) * t, :],
                         lambda j: k_ref[0, 0, j * t:(j + 1) * t, :],
                         lambda j: v_ref[0, 0, j * t:(j + 1) * t, :], s_scr, None)
        o_ref[0, qi * t:(qi + 1) * t, :] = o.astype(BF16)


def _mla_attn(q, k, v):
    t = TQ_MLA
    head = lambda d: pl.BlockSpec((1, 1, SEQ, d), lambda b, h: (b, h, 0, 0))
    return pl.pallas_call(
        _mla_attn_body,
        grid=(BATCH, MLA_HEADS),
        in_specs=[head(MLA_QK), head(MLA_QK), head(MLA_V)],
        out_specs=pl.BlockSpec((1, SEQ, MLA_V), lambda b, h: (b, 0, h)),
        out_shape=jax.ShapeDtypeStruct((BATCH, SEQ, MLA_HEADS * MLA_V), BF16),
        scratch_shapes=[pltpu.VMEM((t, N_SCORE_CHUNKS * t), F32)],
        compiler_params=_params("parallel", "parallel"),
        name="mla_attn",
    )(q, k, v)


def _glu_proj_body(x_ref, gmix_ref, wa_ref, wg_ref, z_ref, h_scr):
    @pl.when(pl.program_id(1) == 0)
    def _():
        h_scr[...] = _rms_bf16(x_ref[...], gmix_ref[...])

    h = h_scr[...]
    a = _dot(h, wa_ref[...])
    g = _dot(h, wg_ref[...])
    z_ref[...] = a * jax.nn.sigmoid(g)


def _glu_proj(x, gmix, w_in_b, l):
    tm, tn = TM_PROJ, TN_CONV
    return pl.pallas_call(
        _glu_proj_body,
        grid=(TOKENS // tm, CONV_CH // tn),
        in_specs=[
            pl.BlockSpec((tm, D_MODEL), lambda i, j: (i, 0)),
            pl.BlockSpec((1, D_MODEL), lambda i, j: (0, 0)),
            pl.BlockSpec((None, D_MODEL, tn), lambda i, j: (l, 0, COL_CONV_A // tn + j)),
            pl.BlockSpec((None, D_MODEL, tn), lambda i, j: (l, 0, COL_CONV_G // tn + j)),
        ],
        out_specs=pl.BlockSpec((tm, tn), lambda i, j: (i, j)),
        out_shape=jax.ShapeDtypeStruct((TOKENS, CONV_CH), F32),
        scratch_shapes=[pltpu.VMEM((tm, D_MODEL), BF16)],
        compiler_params=_params("parallel", "arbitrary"),
        name="glu_proj",
    )(x, gmix, w_in_b, w_in_b)


def _conv_body(z_ref, w_ref, b_ref, lg_ref, lb_ref, o_ref, zs):
    ts, halo = TS_CONV, CONV_HALO
    si = pl.program_id(1)

    @pl.when(si == 0)
    def _():
        zs[0, 0:halo, :] = jnp.zeros((halo, CONV_CH), F32)

    @pl.when(si > 0)
    def _():
        zs[0, 0:halo, :] = zs[0, ts:ts + halo, :]

    zs[0, halo:halo + ts, :] = z_ref[...]
    n_shifted = ts + halo - SUBLANES
    for s in range(1, SUBLANES):
        zs[s, 0:n_shifted, :] = zs[0, s:s + n_shifted, :]
    shift = halo - (CONV_WIDTH - 1)
    for r0 in range(0, ts, CONV_ROWS):
        parts = []
        for c0 in range(0, CONV_CH, CONV_LANES):
            acc = jnp.zeros((CONV_ROWS, CONV_LANES), F32) + b_ref[:, c0:c0 + CONV_LANES]
            for k in range(CONV_WIDTH):
                s = (shift + k) % SUBLANES
                a = r0 + shift + k - s
                acc = acc + zs[s, a:a + CONV_ROWS, c0:c0 + CONV_LANES] * w_ref[k:k + 1, c0:c0 + CONV_LANES]
            parts.append(acc)
        acc = jnp.concatenate(parts, axis=-1)
        mu = jnp.mean(acc, axis=-1, keepdims=True)
        xc = acc - mu
        var = jnp.mean(xc * xc, axis=-1, keepdims=True)
        y = xc * lax.rsqrt(var + NORM_EPS) * lg_ref[...] + lb_ref[...]
        o_ref[r0:r0 + CONV_ROWS, :] = (y * jax.nn.sigmoid(y)).astype(BF16)


def _conv(z, w_dw, b_dw, ln_g, ln_b):
    ts = TS_CONV
    n_s = SEQ // ts
    vec = pl.BlockSpec((1, CONV_CH), lambda b, s: (0, 0))
    return pl.pallas_call(
        _conv_body,
        grid=(BATCH, n_s),
        in_specs=[
            pl.BlockSpec((ts, CONV_CH), lambda b, s: (b * n_s + s, 0)),
            pl.BlockSpec((CONV_WIDTH, CONV_CH), lambda b, s: (0, 0)),
            vec, vec, vec,
        ],
        out_specs=pl.BlockSpec((ts, CONV_CH), lambda b, s: (b * n_s + s, 0)),
        out_shape=jax.ShapeDtypeStruct((TOKENS, CONV_CH), BF16),
        scratch_shapes=[pltpu.VMEM((SUBLANES, ts + CONV_HALO, CONV_CH), F32)],
        compiler_params=_params("arbitrary", "arbitrary"),
        name="conv",
    )(z, w_dw, b_dw.reshape(1, -1), ln_g.reshape(1, -1), ln_b.reshape(1, -1))


def _moba_proj_body(x_ref, gmix_ref, w_ref, g_ref, o_ref, h_scr):
    j = pl.program_id(1)

    @pl.when(j == 0)
    def _():
        h_scr[...] = _rms_bf16(x_ref[...], gmix_ref[...])

    u = _dot(h_scr[...], w_ref[...])
    g = g_ref[0]
    for h in range(MOBA_HEADS):
        uh = u[:, h * MOBA_HEAD_DIM:(h + 1) * MOBA_HEAD_DIM]
        ms = jnp.mean(uh * uh, axis=-1, keepdims=True)
        normed = uh * lax.rsqrt(ms + NORM_EPS) * g
        o_ref[0, 0, h] = jnp.where(j < 2, normed, uh)


def _moba_proj(x, gmix, w_in_b, l, gq, gk):
    tm = TM_PROJ
    n_s = SEQ // tm
    hd = MOBA_HEADS * MOBA_HEAD_DIM
    g = jnp.stack([gq, gk, jnp.ones_like(gq)]).reshape(3, 1, MOBA_HEAD_DIM)
    return pl.pallas_call(
        _moba_proj_body,
        grid=(TOKENS // tm, 3),
        in_specs=[
            pl.BlockSpec((tm, D_MODEL), lambda i, j: (i, 0)),
            pl.BlockSpec((1, D_MODEL), lambda i, j: (0, 0)),
            pl.BlockSpec((None, D_MODEL, hd), lambda i, j: (l, 0, COL_MOBA // hd + j)),
            pl.BlockSpec((1, 1, MOBA_HEAD_DIM), lambda i, j: (j, 0, 0)),
        ],
        out_specs=pl.BlockSpec((1, 1, MOBA_HEADS, tm, MOBA_HEAD_DIM),
                               lambda i, j: (j, i // n_s, 0, i % n_s, 0)),
        out_shape=jax.ShapeDtypeStruct((3, BATCH, MOBA_HEADS, SEQ, MOBA_HEAD_DIM), F32),
        scratch_shapes=[pltpu.VMEM((tm, D_MODEL), BF16)],
        compiler_params=_params("parallel", "arbitrary"),
        name="moba_proj",
    )(x, gmix, w_in_b, g)


def _moba_attn_body(slope_ref, q_ref, k_ref, v_ref, o_ref, kb_scr, vb_scr, s_scr):
    blk, nb = MOBA_BLOCK, MOBA_NB
    kb_scr[...] = k_ref[0, 0, 0].astype(BF16)
    vb_scr[...] = v_ref[0, 0, 0].astype(BF16)
    km = jnp.concatenate(
        [jnp.mean(k_ref[0, 0, 0, j * blk:(j + 1) * blk, :], axis=0, keepdims=True) for j in range(nb)], axis=0)
    slope = slope_ref[0][:, 0:1] * LOG2_E
    key_pos = lax.broadcasted_iota(jnp.int32, (1, blk), 1).astype(F32)
    lane = lax.broadcasted_iota(jnp.int32, (blk, nb), 1)

    for n in range(nb):
        q = q_ref[0, 0, 0, n * blk:(n + 1) * blk, :]
        row_bias = None
        if n > MOBA_TOPK:
            cols = []
            gate = jnp.full((blk, nb), NEG_INF, F32)
            for j in range(nb):
                if j < n:
                    gj = jnp.sum(q * km[j:j + 1, :], axis=-1, keepdims=True)
                    gate = jnp.where(lane == j, gj, gate)
                else:
                    gj = jnp.full((blk, 1), NEG_INF, F32)
                cols.append(gj)
            rank = jnp.zeros((blk, nb), jnp.int32)
            for j in range(nb):
                wins = jnp.where(cols[j] > gate, 1, 0)
                wins_or_ties = jnp.where(cols[j] >= gate, 1, 0)
                rank = rank + jnp.where(lane > j, wins_or_ties, wins)
            row_bias = jnp.where((rank < MOBA_TOPK) & (lane < n), 0.0, NEG_INF)

        def bias_fn(j, s, n=n, row_bias=row_bias):
            s = s + slope * (key_pos + float(j * blk))
            if row_bias is not None and j < n:
                s = s + row_bias[:, j:j + 1]
            return s

        qb = (q * (MOBA_HEAD_DIM ** -0.5 * LOG2_E)).astype(BF16)
        o = _attend_tile(n, qb, lambda j: kb_scr[j * blk:(j + 1) * blk, :],
                         lambda j: vb_scr[j * blk:(j + 1) * blk, :], s_scr, bias_fn)
        o_ref[0, n * blk:(n + 1) * blk, :] = o.astype(BF16)


def _moba_attn(qkv, slopes):
    blk = MOBA_BLOCK
    assert blk == TQ_MLA
    spec = lambda which: pl.BlockSpec((1, 1, 1, SEQ, MOBA_HEAD_DIM), lambda b, h: (which, b, h, 0, 0))
    return pl.pallas_call(
        _moba_attn_body,
        grid=(BATCH, MOBA_HEADS),
        in_specs=[pl.BlockSpec((1, 1, 128), lambda b, h: (h, 0, 0)), spec(0), spec(1), spec(2)],
        out_specs=pl.BlockSpec((1, SEQ, MOBA_HEAD_DIM), lambda b, h: (b, 0, h)),
        out_shape=jax.ShapeDtypeStruct((BATCH, SEQ, MOBA_HEADS * MOBA_HEAD_DIM), BF16),
        scratch_shapes=[pltpu.VMEM((SEQ, MOBA_HEAD_DIM), BF16), pltpu.VMEM((SEQ, MOBA_HEAD_DIM), BF16),
                        pltpu.VMEM((blk, N_SCORE_CHUNKS * blk), F32)],
        compiler_params=_params("parallel", "parallel"),
        name="moba_attn",
    )(slopes, qkv, qkv, qkv)


def _merge_body(x_ref, gmix_ref, oa_ref, ob_ref, oc_ref, wga_ref, wgb_ref, wgc_ref, bga_ref, bgb_ref, bgc_ref,
                woa_ref, wob_ref, woc_ref, wout_ref, o_ref, h_scr, acc_scr):
    j = pl.program_id(1)

    @pl.when(j == 0)
    def _():
        h_scr[...] = _rms_bf16(x_ref[...], gmix_ref[...])
        acc_scr[...] = jnp.zeros_like(acc_scr)

    h = h_scr[...]
    merged = None
    for o_i, wg_i, bg_i, wo_i in ((oa_ref, wga_ref, bga_ref, woa_ref), (ob_ref, wgb_ref, bgb_ref, wob_ref),
                                  (oc_ref, wgc_ref, bgc_ref, woc_ref)):
        gate = jax.nn.sigmoid(_dot(h, wg_i[...]) + bg_i[...])
        term = gate * _dot(o_i[...], wo_i[...])
        merged = term if merged is None else merged + term
    acc_scr[...] += _dot(merged.astype(BF16), wout_ref[...])

    @pl.when(j == pl.num_programs(1) - 1)
    def _():
        o_ref[...] = x_ref[...] + acc_scr[...]


def _merge(x, gmix, oa, ob, oc, w_in_b, b_gate, w_oa, w_ob, w_oc, w_out, l):
    tm, tn = TM_MERGE, TN_MERGE
    n_j = D_MODEL // tn
    tok = lambda d: pl.BlockSpec((tm, d), lambda i, j: (i, 0))
    wg = lambda b: pl.BlockSpec((None, D_MODEL, tn), lambda i, j: (l, 0, COL_GATE // tn + b * n_j + j))
    bg = lambda b: pl.BlockSpec((None, 1, tn), lambda i, j: (l, 0, b * n_j + j))
    wo = lambda d: pl.BlockSpec((None, d, tn), lambda i, j: (l, 0, j))
    return pl.pallas_call(
        _merge_body,
        grid=(TOKENS // tm, n_j),
        in_specs=[
            tok(D_MODEL),
            pl.BlockSpec((1, D_MODEL), lambda i, j: (0, 0)),
            tok(MLA_HEADS * MLA_V), tok(CONV_CH), tok(MOBA_HEADS * MOBA_HEAD_DIM),
            wg(0), wg(1), wg(2), bg(0), bg(1), bg(2),
            wo(MLA_HEADS * MLA_V), wo(CONV_CH), wo(MOBA_HEADS * MOBA_HEAD_DIM),
            pl.BlockSpec((None, tn, D_MODEL), lambda i, j: (l, j, 0)),
        ],
        out_specs=tok(D_MODEL),
        out_shape=jax.ShapeDtypeStruct((TOKENS, D_MODEL), F32),
        scratch_shapes=[pltpu.VMEM((tm, D_MODEL), BF16), pltpu.VMEM((tm, D_MODEL), F32)],
        compiler_params=_params("parallel", "arbitrary"),
        name="merge",
    )(x, gmix, oa, ob, oc, w_in_b, w_in_b, w_in_b, b_gate, b_gate, b_gate, w_oa, w_ob, w_oc, w_out)


def _rope_table():
    half = MLA_ROPE // 2
    inv_freq = jnp.exp(-math.log(ROPE_THETA) * jnp.arange(half, dtype=F32) * 2.0 / MLA_ROPE)
    ang = jnp.arange(SEQ, dtype=jnp.int32).astype(F32)[:, None] * inv_freq[None, :]
    cos, sin = jnp.cos(ang), jnp.sin(ang)
    return jnp.concatenate([cos, cos, sin, sin], axis=-1)


def kernel(x, ffn1_norm, ffn1_w_gate, ffn1_w_up, ffn1_w_down, mix_norm, w_in, b_gate, mla_cq_norm, mla_ckv_norm, mla_w_uq, mla_w_ukv, mla_q_norm, mla_k_norm, mla_w_o, conv_w_dw, conv_b_dw, conv_ln_g, conv_ln_b, conv_w_pw, moba_q_norm, moba_k_norm, moba_w_o, w_out, ffn2_norm, ffn2_w_gate, ffn2_w_up, ffn2_w_down):
    cs = _rope_table()
    slopes = jnp.exp2(-8.0 * jnp.arange(1, MOBA_HEADS + 1, dtype=F32) / MOBA_HEADS)
    slopes = jnp.broadcast_to(slopes[:, None, None], (MOBA_HEADS, 1, 128))
    ffn1 = [w.astype(BF16) for w in (ffn1_w_gate, ffn1_w_up, ffn1_w_down)]
    ffn2 = [w.astype(BF16) for w in (ffn2_w_gate, ffn2_w_up, ffn2_w_down)]
    w_in_b, w_lat, w_kk = _prep_w_in(w_in)
    wq = mla_w_uq.reshape(DEPTH, MLA_Q_RANK, MLA_HEADS, MLA_QK)
    wq = jnp.concatenate([wq, _rot_cols(wq[..., MLA_NOPE:])], axis=-1).astype(BF16)
    wq = wq.reshape(DEPTH, MLA_Q_RANK, MLA_HEADS * (MLA_NOPE + 2 * MLA_ROPE))
    wkv = mla_w_ukv.astype(BF16)
    w_oa, w_ob, w_oc, w_o = (w.astype(BF16) for w in (mla_w_o, conv_w_pw, moba_w_o, w_out))
    bg = b_gate.reshape(DEPTH, 1, N_BRANCH * D_MODEL)
    xt = x.reshape(TOKENS, D_MODEL)
    for l in range(DEPTH):
        xt = _ffn(xt, ffn1_norm[l].reshape(1, -1), *ffn1, l)
        gmix = mix_norm[l].reshape(1, -1)
        q, k, v = _mla_proj(xt, gmix, w_lat, w_kk, l, mla_cq_norm[l], mla_ckv_norm[l], wq, wkv,
                            mla_q_norm[l], mla_k_norm[l], cs)
        o_mla = _mla_attn(q, k, v).reshape(TOKENS, MLA_HEADS * MLA_V)
        z = _glu_proj(xt, gmix, w_in_b, l)
        o_conv = _conv(z, conv_w_dw[l], conv_b_dw[l], conv_ln_g[l], conv_ln_b[l])
        qkv = _moba_proj(xt, gmix, w_in_b, l, moba_q_norm[l], moba_k_norm[l])
        o_moba = _moba_attn(qkv, slopes).reshape(TOKENS, MOBA_HEADS * MOBA_HEAD_DIM)
        xt = _merge(xt, gmix, o_mla, o_conv, o_moba, w_in_b, bg, w_oa, w_ob, w_oc, w_o, l)
        xt = _ffn(xt, ffn2_norm[l].reshape(1, -1), *ffn2, l)
    return xt.reshape(BATCH, SEQ, D_MODEL)
```

```python
import math

import jax
import jax.numpy as jnp
from jax import lax
from jax.experimental import pallas as pl
from jax.experimental.pallas import tpu as pltpu

D_MODEL = 2048
BATCH = 4
SEQ = 2048
DEPTH = 2
TOKENS = BATCH * SEQ

MLA_HEADS = 8
MLA_Q_RANK = 768
MLA_KV_RANK = 512
MLA_NOPE = 128
MLA_ROPE = 64
MLA_QK = MLA_NOPE + MLA_ROPE
MLA_V = 128
ROPE_THETA = 10000.0

CONV_CH = 1024
CONV_WIDTH = 31
CONV_HALO = 32

MOBA_HEADS = 8
MOBA_HEAD_DIM = 128
MOBA_BLOCK = 256
MOBA_TOPK = 3
MOBA_NB = SEQ // MOBA_BLOCK

D_FF = 5632
N_BRANCH = 3
NORM_EPS = 1e-6
NEG_INF = -1e30
LOG2_E = math.log2(math.e)

OFF_CQ = 0
OFF_CKV = OFF_CQ + MLA_Q_RANK
OFF_KR = OFF_CKV + MLA_KV_RANK
OFF_CONV = OFF_KR + MLA_ROPE
OFF_MOBA = OFF_CONV + 2 * CONV_CH
OFF_GATE = OFF_MOBA + 3 * MOBA_HEADS * MOBA_HEAD_DIM

MLA_LATENT = MLA_Q_RANK + MLA_KV_RANK
MAIN_COLS = N_BRANCH * D_MODEL + OFF_GATE - OFF_CONV
COL_CONV_A = 0
COL_CONV_G = COL_CONV_A + CONV_CH
COL_MOBA = OFF_MOBA - OFF_CONV
COL_GATE = OFF_GATE - OFF_CONV
TR_PREP = 256
PREP_CHUNK = 256

F32 = jnp.float32
BF16 = jnp.bfloat16

VMEM_LIMIT_BYTES = 56 * 1024 * 1024

TM_FFN = 1024
TF_FFN = 256
TM_PROJ = 512
MLA_PROJ_ROWS = 128
TN_CONV = 512
TM_MERGE = 512
TN_MERGE = 512
TQ_MLA = 256
TS_CONV = 256
CONV_ROWS = 64
CONV_LANES = 256
SUBLANES = 8


def _params(*semantics):
    return pltpu.CompilerParams(dimension_semantics=semantics, vmem_limit_bytes=VMEM_LIMIT_BYTES)


def _rms_bf16(x, g):
    ms = jnp.mean(x * x, axis=-1, keepdims=True)
    return (x * lax.rsqrt(ms + NORM_EPS) * g).astype(BF16)


def _dot(a, b):
    return jnp.dot(a, b, preferred_element_type=F32)


def _dot_nt(a, b):
    return lax.dot_general(a, b, (((1,), (1,)), ((), ())), preferred_element_type=F32)


def _rot_cols(w):
    half = w.shape[-1] // 2
    return jnp.concatenate([-w[..., half:], w[..., :half]], axis=-1)


def _swap_halves(g):
    half = g.shape[-1] // 2
    return jnp.concatenate([g[..., half:], g[..., :half]], axis=-1)


def _prep_w_in_body(wt_ref, main_ref, lat_ref, kk_ref):
    def put(dst_ref, src0, n_cols):
        for c in range(0, n_cols, PREP_CHUNK):
            dst_ref[:, c:c + PREP_CHUNK] = wt_ref[src0 + c:src0 + c + PREP_CHUNK, :].T.astype(BF16)

    put(main_ref, OFF_CONV, MAIN_COLS)
    put(lat_ref, OFF_CQ, MLA_LATENT)
    v = wt_ref[OFF_KR:OFF_KR + 2 * MLA_ROPE, :].T
    lane = lax.broadcasted_iota(jnp.int32, v.shape, 1)
    half = MLA_ROPE // 2
    kk = jnp.where(lane < MLA_ROPE, v,
                   jnp.where(lane < MLA_ROPE + half, -pltpu.roll(v, half, 1), pltpu.roll(v, MLA_ROPE + half, 1)))
    kk_ref[...] = kk.astype(BF16)


def _prep_w_in(w_in):
    d_in = w_in.shape[-1]
    wt = jnp.swapaxes(w_in, 1, 2)
    spec = lambda n: pl.BlockSpec((None, TR_PREP, n), lambda l, i: (l, i, 0))
    return pl.pallas_call(
        _prep_w_in_body,
        grid=(DEPTH, D_MODEL // TR_PREP),
        in_specs=[pl.BlockSpec((None, d_in, TR_PREP), lambda l, i: (l, 0, i))],
        out_specs=[spec(MAIN_COLS), spec(MLA_LATENT), spec(2 * MLA_ROPE)],
        out_shape=[jax.ShapeDtypeStruct((DEPTH, D_MODEL, MAIN_COLS), BF16),
                   jax.ShapeDtypeStruct((DEPTH, D_MODEL, MLA_LATENT), BF16),
                   jax.ShapeDtypeStruct((DEPTH, D_MODEL, 2 * MLA_ROPE), BF16)],
        compiler_params=_params("parallel", "parallel"),
        name="prep_w_in",
    )(wt)


def _ffn_body(x_ref, g_ref, wg_ref, wu_ref, wd_ref, o_ref, h_scr):
    j = pl.program_id(1)

    @pl.when(j == 0)
    def _():
        h_scr[...] = _rms_bf16(x_ref[...], g_ref[...])
        o_ref[...] = jnp.zeros_like(o_ref)

    h = h_scr[...]
    a = _dot(h, wg_ref[...].astype(BF16))
    b = _dot(h, wu_ref[...].astype(BF16))
    t = (a * jax.nn.sigmoid(a) * b).astype(BF16)
    o_ref[...] += _dot(t, wd_ref[...].astype(BF16))

    @pl.when(j == pl.num_programs(1) - 1)
    def _():
        o_ref[...] = x_ref[...] + 0.5 * o_ref[...]


def _ffn(x, g, wg, wu, wd, l):
    tm, tf = TM_FFN, TF_FFN
    return pl.pallas_call(
        _ffn_body,
        grid=(TOKENS // tm, D_FF // tf),
        in_specs=[
            pl.BlockSpec((tm, D_MODEL), lambda i, j: (i, 0)),
            pl.BlockSpec((1, D_MODEL), lambda i, j: (0, 0)),
            pl.BlockSpec((None, D_MODEL, tf), lambda i, j: (l, 0, j)),
            pl.BlockSpec((None, D_MODEL, tf), lambda i, j: (l, 0, j)),
            pl.BlockSpec((None, tf, D_MODEL), lambda i, j: (l, j, 0)),
        ],
        out_specs=pl.BlockSpec((tm, D_MODEL), lambda i, j: (i, 0)),
        out_shape=jax.ShapeDtypeStruct((TOKENS, D_MODEL), F32),
        scratch_shapes=[pltpu.VMEM((tm, D_MODEL), BF16)],
        compiler_params=_params("parallel", "arbitrary"),
        name="ffn",
    )(x, g, wg, wu, wd)


def _rope_pair(t, table):
    u = t * table
    return (u + pltpu.roll(u, MLA_ROPE, 1))[:, :MLA_ROPE]


def _mla_proj_body(x_ref, gmix_ref, wc_ref, wkk_ref, gcq_ref, gckv_ref, wq_ref, wkv_ref,
                   gqn_ref, g2q_ref, gkn_ref, g2k_ref, cs_ref, q_ref, k_ref, v_ref):
    for r0 in range(0, x_ref.shape[0], MLA_PROJ_ROWS):
        _mla_proj_rows(slice(r0, r0 + MLA_PROJ_ROWS), x_ref, gmix_ref, wc_ref, wkk_ref, gcq_ref, gckv_ref,
                       wq_ref, wkv_ref, gqn_ref, g2q_ref, gkn_ref, g2k_ref, cs_ref, q_ref, k_ref, v_ref)


def _mla_proj_rows(rows, x_ref, gmix_ref, wc_ref, wkk_ref, gcq_ref, gckv_ref, wq_ref, wkv_ref,
                   gqn_ref, g2q_ref, gkn_ref, g2k_ref, cs_ref, q_ref, k_ref, v_ref):
    hn = _rms_bf16(x_ref[rows, :], gmix_ref[...])
    c = _dot(hn, wc_ref[...])
    cqn = _rms_bf16(c[:, :MLA_Q_RANK], gcq_ref[...])
    ckvn = _rms_bf16(c[:, MLA_Q_RANK:], gckv_ref[...])
    kk = _dot(hn, wkk_ref[...])
    low = lax.broadcasted_iota(jnp.int32, kk.shape, 1) < MLA_ROPE
    cs = cs_ref[rows, :]
    kr_ss = jnp.sum(jnp.where(low, kk * kk, 0.0), axis=-1, keepdims=True)
    kr_rot = _rope_pair(kk, cs * g2k_ref[...])
    cs_q = cs * g2q_ref[...]
    q_scale = MLA_QK ** -0.5 * LOG2_E
    hw = MLA_NOPE + 2 * MLA_ROPE
    for h in range(MLA_HEADS):
        rq = _dot(cqn, wq_ref[:, h * hw:(h + 1) * hw])
        qn, qt = rq[:, :MLA_NOPE], rq[:, MLA_NOPE:]
        ss = (jnp.sum(qn * qn, axis=-1, keepdims=True)
              + jnp.sum(jnp.where(low, qt * qt, 0.0), axis=-1, keepdims=True)) / MLA_QK
        r = lax.rsqrt(ss + NORM_EPS) * q_scale
        q_ref[0, h, rows, :MLA_NOPE] = (qn * r * gqn_ref[...]).astype(BF16)
        q_ref[0, h, rows, MLA_NOPE:] = (_rope_pair(qt, cs_q) * r).astype(BF16)
        rkv = _dot(ckvn, wkv_ref[:, h * hw:(h + 1) * hw])
        kn = rkv[:, :MLA_NOPE]
        ss = (jnp.sum(kn * kn, axis=-1, keepdims=True) + kr_ss) / MLA_QK
        r = lax.rsqrt(ss + NORM_EPS)
        k_ref[0, h, rows, :MLA_NOPE] = (kn * r * gkn_ref[...]).astype(BF16)
        k_ref[0, h, rows, MLA_NOPE:] = (kr_rot * r).astype(BF16)
        v_ref[0, h, rows, :] = rkv[:, MLA_NOPE:].astype(BF16)


def _mla_proj(x, gmix, w_lat, w_kk, l, gcq, gckv, wq, wkv, gq, gk, cs):
    tm = TM_PROJ
    n_s = SEQ // tm
    gq = gq.reshape(1, MLA_QK)
    gk = gk.reshape(1, MLA_QK)
    pair = lambda g: jnp.concatenate([g[:, MLA_NOPE:], _swap_halves(g[:, MLA_NOPE:])], axis=-1)
    vec = lambda n: pl.BlockSpec((1, n), lambda i: (0, 0))
    head_spec = lambda d: pl.BlockSpec((1, MLA_HEADS, tm, d), lambda i: (i // n_s, 0, i % n_s, 0))
    return pl.pallas_call(
        _mla_proj_body,
        grid=(TOKENS // tm,),
        in_specs=[
            pl.BlockSpec((tm, D_MODEL), lambda i: (i, 0)),
            vec(D_MODEL),
            pl.BlockSpec((None, D_MODEL, MLA_LATENT), lambda i: (l, 0, 0)),
            pl.BlockSpec((None, D_MODEL, 2 * MLA_ROPE), lambda i: (l, 0, 0)),
            vec(MLA_Q_RANK), vec(MLA_KV_RANK),
            pl.BlockSpec((None,) + wq.shape[1:], lambda i: (l, 0, 0)),
            pl.BlockSpec((None,) + wkv.shape[1:], lambda i: (l, 0, 0)),
            vec(MLA_NOPE), vec(2 * MLA_ROPE), vec(MLA_NOPE), vec(2 * MLA_ROPE),
            pl.BlockSpec((tm, 2 * MLA_ROPE), lambda i: (i % n_s, 0)),
        ],
        out_specs=[head_spec(MLA_QK), head_spec(MLA_QK), head_spec(MLA_V)],
        out_shape=[jax.ShapeDtypeStruct((BATCH, MLA_HEADS, SEQ, MLA_QK), BF16),
                   jax.ShapeDtypeStruct((BATCH, MLA_HEADS, SEQ, MLA_QK), BF16),
                   jax.ShapeDtypeStruct((BATCH, MLA_HEADS, SEQ, MLA_V), BF16)],
        compiler_params=_params("parallel"),
        name="mla_proj",
    )(x, gmix, w_lat, w_kk, gcq.reshape(1, -1), gckv.reshape(1, -1), wq, wkv,
      gq[:, :MLA_NOPE], pair(gq), gk[:, :MLA_NOPE], pair(gk), cs)


N_ATTN_TILES = SEQ // TQ_MLA
N_SCORE_CHUNKS = N_ATTN_TILES * (N_ATTN_TILES + 1) // 2


def _attend_tile(qi, q, k_chunk, v_chunk, s_scr, bias_fn):
    t = TQ_MLA
    base = qi * (qi + 1) // 2
    row = lax.broadcasted_iota(jnp.int32, (t, t), 0)
    col = lax.broadcasted_iota(jnp.int32, (t, t), 1)
    mx = None
    for j in range(qi + 1):
        s = _dot_nt(q, k_chunk(j))
        if bias_fn is not None:
            s = bias_fn(j, s)
        if j == qi:
            s = jnp.where(col <= row, s, NEG_INF)
        s_scr[:, (base + j) * t:(base + j + 1) * t] = s
        mx = s if mx is None else jnp.maximum(mx, s)
    m = jnp.max(mx, axis=-1, keepdims=True)
    lsum = None
    acc = None
    for j in range(qi + 1):
        p = jnp.exp2(s_scr[:, (base + j) * t:(base + j + 1) * t] - m)
        lsum = p if lsum is None else lsum + p
        pv = _dot(p.astype(BF16), v_chunk(j))
        acc = pv if acc is None else acc + pv
    return acc / jnp.sum(lsum, axis=-1, keepdims=True)


def _mla_attn_body(q_ref, k_ref, v_ref, o_ref, s_scr):
    t = TQ_MLA
    for qi in range(N_ATTN_TILES):
        o = _attend_tile(qi, q_ref[0, 0, qi * t:(qi + 1) * t, :],
                         lambda j: k_ref[0, 0, j * t:(j + 1) * t, :],
                         lambda j: v_ref[0, 0, j * t:(j + 1) * t, :], s_scr, None)
        o_ref[0, qi * t:(qi + 1) * t, :] = o.astype(BF16)


def _mla_attn(q, k, v):
    t = TQ_MLA
    head = lambda d: pl.BlockSpec((1, 1, SEQ, d), lambda b, h: (b, h, 0, 0))
    return pl.pallas_call(
        _mla_attn_body,
        grid=(BATCH, MLA_HEADS),
        in_specs=[head(MLA_QK), head(MLA_QK), head(MLA_V)],
        out_specs=pl.BlockSpec((1, SEQ, MLA_V), lambda b, h: (b, 0, h)),
        out_shape=jax.ShapeDtypeStruct((BATCH, SEQ, MLA_HEADS * MLA_V), BF16),
        scratch_shapes=[pltpu.VMEM((t, N_SCORE_CHUNKS * t), F32)],
        compiler_params=_params("parallel", "parallel"),
        name="mla_attn",
    )(q, k, v)


def _glu_proj_body(x_ref, gmix_ref, wa_ref, wg_ref, z_ref, h_scr):
    @pl.when(pl.program_id(1) == 0)
    def _():
        h_scr[...] = _rms_bf16(x_ref[...], gmix_ref[...])

    h = h_scr[...]
    a = _dot(h, wa_ref[...])
    g = _dot(h, wg_ref[...])
    z_ref[...] = a * jax.nn.sigmoid(g)


def _glu_proj(x, gmix, w_in_b, l):
    tm, tn = TM_PROJ, TN_CONV
    return pl.pallas_call(
        _glu_proj_body,
        grid=(TOKENS // tm, CONV_CH // tn),
        in_specs=[
            pl.BlockSpec((tm, D_MODEL), lambda i, j: (i, 0)),
            pl.BlockSpec((1, D_MODEL), lambda i, j: (0, 0)),
            pl.BlockSpec((None, D_MODEL, tn), lambda i, j: (l, 0, COL_CONV_A // tn + j)),
            pl.BlockSpec((None, D_MODEL, tn), lambda i, j: (l, 0, COL_CONV_G // tn + j)),
        ],
        out_specs=pl.BlockSpec((tm, tn), lambda i, j: (i, j)),
        out_shape=jax.ShapeDtypeStruct((TOKENS, CONV_CH), F32),
        scratch_shapes=[pltpu.VMEM((tm, D_MODEL), BF16)],
        compiler_params=_params("parallel", "arbitrary"),
        name="glu_proj",
    )(x, gmix, w_in_b, w_in_b)


def _conv_body(z_ref, w_ref, b_ref, lg_ref, lb_ref, o_ref, zs):
    ts, halo = TS_CONV, CONV_HALO
    si = pl.program_id(1)

    @pl.when(si == 0)
    def _():
        zs[0, 0:halo, :] = jnp.zeros((halo, CONV_CH), F32)

    @pl.when(si > 0)
    def _():
        zs[0, 0:halo, :] = zs[0, ts:ts + halo, :]

    zs[0, halo:halo + ts, :] = z_ref[...]
    n_shifted = ts + halo - SUBLANES
    for s in range(1, SUBLANES):
        zs[s, 0:n_shifted, :] = zs[0, s:s + n_shifted, :]
    shift = halo - (CONV_WIDTH - 1)
    for r0 in range(0, ts, CONV_ROWS):
        parts = []
        for c0 in range(0, CONV_CH, CONV_LANES):
            acc = jnp.zeros((CONV_ROWS, CONV_LANES), F32) + b_ref[:, c0:c0 + CONV_LANES]
            for k in range(CONV_WIDTH):
                s = (shift + k) % SUBLANES
                a = r0 + shift + k - s
                acc = acc + zs[s, a:a + CONV_ROWS, c0:c0 + CONV_LANES] * w_ref[k:k + 1, c0:c0 + CONV_LANES]
            parts.append(acc)
        acc = jnp.concatenate(parts, axis=-1)
        mu = jnp.mean(acc, axis=-1, keepdims=True)
        xc = acc - mu
        var = jnp.mean(xc * xc, axis=-1, keepdims=True)
        y = xc * lax.rsqrt(var + NORM_EPS) * lg_ref[...] + lb_ref[...]
        o_ref[r0:r0 + CONV_ROWS, :] = (y * jax.nn.sigmoid(y)).astype(BF16)


def _conv(z, w_dw, b_dw, ln_g, ln_b):
    ts = TS_CONV
    n_s = SEQ // ts
    vec = pl.BlockSpec((1, CONV_CH), lambda b, s: (0, 0))
    return pl.pallas_call(
        _conv_body,
        grid=(BATCH, n_s),
        in_specs=[
            pl.BlockSpec((ts, CONV_CH), lambda b, s: (b * n_s + s, 0)),
            pl.BlockSpec((CONV_WIDTH, CONV_CH), lambda b, s: (0, 0)),
            vec, vec, vec,
        ],
        out_specs=pl.BlockSpec((ts, CONV_CH), lambda b, s: (b * n_s + s, 0)),
        out_shape=jax.ShapeDtypeStruct((TOKENS, CONV_CH), BF16),
        scratch_shapes=[pltpu.VMEM((SUBLANES, ts + CONV_HALO, CONV_CH), F32)],
        compiler_params=_params("arbitrary", "arbitrary"),
        name="conv",
    )(z, w_dw, b_dw.reshape(1, -1), ln_g.reshape(1, -1), ln_b.reshape(1, -1))


def _moba_proj_body(x_ref, gmix_ref, w_ref, g_ref, o_ref, h_scr):
    j = pl.program_id(1)

    @pl.when(j == 0)
    def _():
        h_scr[...] = _rms_bf16(x_ref[...], gmix_ref[...])

    u = _dot(h_scr[...], w_ref[...])
    g = g_ref[0]
    for h in range(MOBA_HEADS):
        uh = u[:, h * MOBA_HEAD_DIM:(h + 1) * MOBA_HEAD_DIM]
        ms = jnp.mean(uh * uh, axis=-1, keepdims=True)
        normed = uh * lax.rsqrt(ms + NORM_EPS) * g
        o_ref[0, 0, h] = jnp.where(j < 2, normed, uh)


def _moba_proj(x, gmix, w_in_b, l, gq, gk):
    tm = TM_PROJ
    n_s = SEQ // tm
    hd = MOBA_HEADS * MOBA_HEAD_DIM
    g = jnp.stack([gq, gk, jnp.ones_like(gq)]).reshape(3, 1, MOBA_HEAD_DIM)
    return pl.pallas_call(
        _moba_proj_body,
        grid=(TOKENS // tm, 3),
        in_specs=[
            pl.BlockSpec((tm, D_MODEL), lambda i, j: (i, 0)),
            pl.BlockSpec((1, D_MODEL), lambda i, j: (0, 0)),
            pl.BlockSpec((None, D_MODEL, hd), lambda i, j: (l, 0, COL_MOBA // hd + j)),
            pl.BlockSpec((1, 1, MOBA_HEAD_DIM), lambda i, j: (j, 0, 0)),
        ],
        out_specs=pl.BlockSpec((1, 1, MOBA_HEADS, tm, MOBA_HEAD_DIM),
                               lambda i, j: (j, i // n_s, 0, i % n_s, 0)),
        out_shape=jax.ShapeDtypeStruct((3, BATCH, MOBA_HEADS, SEQ, MOBA_HEAD_DIM), F32),
        scratch_shapes=[pltpu.VMEM((tm, D_MODEL), BF16)],
        compiler_params=_params("parallel", "arbitrary"),
        name="moba_proj",
    )(x, gmix, w_in_b, g)


def _moba_attn_body(slope_ref, q_ref, k_ref, v_ref, o_ref, kb_scr, vb_scr, s_scr):
    blk, nb = MOBA_BLOCK, MOBA_NB
    kb_scr[...] = k_ref[0, 0, 0].astype(BF16)
    vb_scr[...] = v_ref[0, 0, 0].astype(BF16)
    km = jnp.concatenate(
        [jnp.mean(k_ref[0, 0, 0, j * blk:(j + 1) * blk, :], axis=0, keepdims=True) for j in range(nb)], axis=0)
    slope = slope_ref[0][:, 0:1] * LOG2_E
    key_pos = lax.broadcasted_iota(jnp.int32, (1, blk), 1).astype(F32)
    lane = lax.broadcasted_iota(jnp.int32, (blk, nb), 1)

    for n in range(nb):
        q = q_ref[0, 0, 0, n * blk:(n + 1) * blk, :]
        row_bias = None
        if n > MOBA_TOPK:
            cols = []
            gate = jnp.full((blk, nb), NEG_INF, F32)
            for j in range(nb):
                if j < n:
                    gj = jnp.sum(q * km[j:j + 1, :], axis=-1, keepdims=True)
                    gate = jnp.where(lane == j, gj, gate)
                else:
                    gj = jnp.full((blk, 1), NEG_INF, F32)
                cols.append(gj)
            rank = jnp.zeros((blk, nb), jnp.int32)
            for j in range(nb):
                wins = jnp.where(cols[j] > gate, 1, 0)
                wins_or_ties = jnp.where(cols[j] >= gate, 1, 0)
                rank = rank + jnp.where(lane > j, wins_or_ties, wins)
            row_bias = jnp.where((rank < MOBA_TOPK) & (lane < n), 0.0, NEG_INF)

        def bias_fn(j, s, n=n, row_bias=row_bias):
            s = s + slope * (key_pos + float(j * blk))
            if row_bias is not None and j < n:
                s = s + row_bias[:, j:j + 1]
            return s

        qb = (q * (MOBA_HEAD_DIM ** -0.5 * LOG2_E)).astype(BF16)
        o = _attend_tile(n, qb, lambda j: kb_scr[j * blk:(j + 1) * blk, :],
                         lambda j: vb_scr[j * blk:(j + 1) * blk, :], s_scr, bias_fn)
        o_ref[0, n * blk:(n + 1) * blk, :] = o.astype(BF16)


def _moba_attn(qkv, slopes):
    blk = MOBA_BLOCK
    assert blk == TQ_MLA
    spec = lambda which: pl.BlockSpec((1, 1, 1, SEQ, MOBA_HEAD_DIM), lambda b, h: (which, b, h, 0, 0))
    return pl.pallas_call(
        _moba_attn_body,
        grid=(BATCH, MOBA_HEADS),
        in_specs=[pl.BlockSpec((1, 1, 128), lambda b, h: (h, 0, 0)), spec(0), spec(1), spec(2)],
        out_specs=pl.BlockSpec((1, SEQ, MOBA_HEAD_DIM), lambda b, h: (b, 0, h)),
        out_shape=jax.ShapeDtypeStruct((BATCH, SEQ, MOBA_HEADS * MOBA_HEAD_DIM), BF16),
        scratch_shapes=[pltpu.VMEM((SEQ, MOBA_HEAD_DIM), BF16), pltpu.VMEM((SEQ, MOBA_HEAD_DIM), BF16),
                        pltpu.VMEM((blk, N_SCORE_CHUNKS * blk), F32)],
        compiler_params=_params("parallel", "parallel"),
        name="moba_attn",
    )(slopes, qkv, qkv, qkv)


def _merge_body(x_ref, gmix_ref, oa_ref, ob_ref, oc_ref, wga_ref, wgb_ref, wgc_ref, bga_ref, bgb_ref, bgc_ref,
                woa_ref, wob_ref, woc_ref, wout_ref, o_ref, h_scr, acc_scr):
    j = pl.program_id(1)

    @pl.when(j == 0)
    def _():
        h_scr[...] = _rms_bf16(x_ref[...], gmix_ref[...])
        acc_scr[...] = jnp.zeros_like(acc_scr)

    h = h_scr[...]
    merged = None
    for o_i, wg_i, bg_i, wo_i in ((oa_ref, wga_ref, bga_ref, woa_ref), (ob_ref, wgb_ref, bgb_ref, wob_ref),
                                  (oc_ref, wgc_ref, bgc_ref, woc_ref)):
        gate = jax.nn.sigmoid(_dot(h, wg_i[...]) + bg_i[...])
        term = gate * _dot(o_i[...], wo_i[...])
        merged = term if merged is None else merged + term
    acc_scr[...] += _dot(merged.astype(BF16), wout_ref[...])

    @pl.when(j == pl.num_programs(1) - 1)
    def _():
        o_ref[...] = x_ref[...] + acc_scr[...]


def _merge(x, gmix, oa, ob, oc, w_in_b, b_gate, w_oa, w_ob, w_oc, w_out, l):
    tm, tn = TM_MERGE, TN_MERGE
    n_j = D_MODEL // tn
    tok = lambda d: pl.BlockSpec((tm, d), lambda i, j: (i, 0))
    wg = lambda b: pl.BlockSpec((None, D_MODEL, tn), lambda i, j: (l, 0, COL_GATE // tn + b * n_j + j))
    bg = lambda b: pl.BlockSpec((None, 1, tn), lambda i, j: (l, 0, b * n_j + j))
    wo = lambda d: pl.BlockSpec((None, d, tn), lambda i, j: (l, 0, j))
    return pl.pallas_call(
        _merge_body,
        grid=(TOKENS // tm, n_j),
        in_specs=[
            tok(D_MODEL),
            pl.BlockSpec((1, D_MODEL), lambda i, j: (0, 0)),
            tok(MLA_HEADS * MLA_V), tok(CONV_CH), tok(MOBA_HEADS * MOBA_HEAD_DIM),
            wg(0), wg(1), wg(2), bg(0), bg(1), bg(2),
            wo(MLA_HEADS * MLA_V), wo(CONV_CH), wo(MOBA_HEADS * MOBA_HEAD_DIM),
            pl.BlockSpec((None, tn, D_MODEL), lambda i, j: (l, j, 0)),
        ],
        out_specs=tok(D_MODEL),
        out_shape=jax.ShapeDtypeStruct((TOKENS, D_MODEL), F32),
        scratch_shapes=[pltpu.VMEM((tm, D_MODEL), BF16), pltpu.VMEM((tm, D_MODEL), F32)],
        compiler_params=_params("parallel", "arbitrary"),
        name="merge",
    )(x, gmix, oa, ob, oc, w_in_b, w_in_b, w_in_b, b_gate, b_gate, b_gate, w_oa, w_ob, w_oc, w_out)


def _rope_table():
    half = MLA_ROPE // 2
    inv_freq = jnp.exp(-math.log(ROPE_THETA) * jnp.arange(half, dtype=F32) * 2.0 / MLA_ROPE)
    ang = jnp.arange(SEQ, dtype=jnp.int32).astype(F32)[:, None] * inv_freq[None, :]
    cos, sin = jnp.cos(ang), jnp.sin(ang)
    return jnp.concatenate([cos, cos, sin, sin], axis=-1)


def kernel(x, ffn1_norm, ffn1_w_gate, ffn1_w_up, ffn1_w_down, mix_norm, w_in, b_gate, mla_cq_norm, mla_ckv_norm, mla_w_uq, mla_w_ukv, mla_q_norm, mla_k_norm, mla_w_o, conv_w_dw, conv_b_dw, conv_ln_g, conv_ln_b, conv_w_pw, moba_q_norm, moba_k_norm, moba_w_o, w_out, ffn2_norm, ffn2_w_gate, ffn2_w_up, ffn2_w_down):
    cs = _rope_table()
    slopes = jnp.exp2(-8.0 * jnp.arange(1, MOBA_HEADS + 1, dtype=F32) / MOBA_HEADS)
    slopes = jnp.broadcast_to(slopes[:, None, None], (MOBA_HEADS, 1, 128))
    ffn1 = (ffn1_w_gate, ffn1_w_up, ffn1_w_down)
    ffn2 = (ffn2_w_gate, ffn2_w_up, ffn2_w_down)
    w_in_b, w_lat, w_kk = _prep_w_in(w_in)
    wq = mla_w_uq.reshape(DEPTH, MLA_Q_RANK, MLA_HEADS, MLA_QK)
    wq = jnp.concatenate([wq, _rot_cols(wq[..., MLA_NOPE:])], axis=-1).astype(BF16)
    wq = wq.reshape(DEPTH, MLA_Q_RANK, MLA_HEADS * (MLA_NOPE + 2 * MLA_ROPE))
    wkv = mla_w_ukv.astype(BF16)
    w_oa, w_ob, w_oc, w_o = (w.astype(BF16) for w in (mla_w_o, conv_w_pw, moba_w_o, w_out))
    bg = b_gate.reshape(DEPTH, 1, N_BRANCH * D_MODEL)
    xt = x.reshape(TOKENS, D_MODEL)
    for l in range(DEPTH):
        xt = _ffn(xt, ffn1_norm[l].reshape(1, -1), *ffn1, l)
        gmix = mix_norm[l].reshape(1, -1)
        q, k, v = _mla_proj(xt, gmix, w_lat, w_kk, l, mla_cq_norm[l], mla_ckv_norm[l], wq, wkv,
                            mla_q_norm[l], mla_k_norm[l], cs)
        o_mla = _mla_attn(q, k, v).reshape(TOKENS, MLA_HEADS * MLA_V)
        z = _glu_proj(xt, gmix, w_in_b, l)
        o_conv = _conv(z, conv_w_dw[l], conv_b_dw[l], conv_ln_g[l], conv_ln_b[l])
        qkv = _moba_proj(xt, gmix, w_in_b, l, moba_q_norm[l], moba_k_norm[l])
        o_moba = _moba_attn(qkv, slopes).reshape(TOKENS, MOBA_HEADS * MOBA_HEAD_DIM)
        xt = _merge(xt, gmix, o_mla, o_conv, o_moba, w_in_b, bg, w_oa, w_ob, w_oc, w_o, l)
        xt = _ffn(xt, ffn2_norm[l].reshape(1, -1), *ffn2, l)
    return xt.reshape(BATCH, SEQ, D_MODEL)
```

```python
import math

import jax
import jax.numpy as jnp
from jax import lax
from jax.experimental import pallas as pl
from jax.experimental.pallas import tpu as pltpu

D_MODEL = 2048
BATCH = 4
SEQ = 2048
DEPTH = 2
TOKENS = BATCH * SEQ

MLA_HEADS = 8
MLA_Q_RANK = 768
MLA_KV_RANK = 512
MLA_NOPE = 128
MLA_ROPE = 64
MLA_QK = MLA_NOPE + MLA_ROPE
MLA_V = 128
ROPE_THETA = 10000.0

CONV_CH = 1024
CONV_WIDTH = 31
CONV_HALO = 32

MOBA_HEADS = 8
MOBA_HEAD_DIM = 128
MOBA_BLOCK = 256
MOBA_TOPK = 3
MOBA_NB = SEQ // MOBA_BLOCK

D_FF = 5632
N_BRANCH = 3
NORM_EPS = 1e-6
NEG_INF = -1e30
LOG2_E = math.log2(math.e)

OFF_CQ = 0
OFF_CKV = OFF_CQ + MLA_Q_RANK
OFF_KR = OFF_CKV + MLA_KV_RANK
OFF_CONV = OFF_KR + MLA_ROPE
OFF_MOBA = OFF_CONV + 2 * CONV_CH
OFF_GATE = OFF_MOBA + 3 * MOBA_HEADS * MOBA_HEAD_DIM

MLA_LATENT = MLA_Q_RANK + MLA_KV_RANK
MAIN_COLS = N_BRANCH * D_MODEL + OFF_GATE - OFF_CONV
COL_CONV_A = 0
COL_CONV_G = COL_CONV_A + CONV_CH
COL_MOBA = OFF_MOBA - OFF_CONV
COL_GATE = OFF_GATE - OFF_CONV
TR_PREP = 256
PREP_CHUNK = 256

F32 = jnp.float32
BF16 = jnp.bfloat16

VMEM_LIMIT_BYTES = 56 * 1024 * 1024

TM_FFN = 1024
TF_FFN = 256
TM_PROJ = 512
MLA_PROJ_ROWS = 128
PROJ_ROWS = 256
TM_MERGE = 512
TN_MERGE = 512
TQ_MLA = 256
TS_CONV = 256
CONV_ROWS = 64
CONV_LANES = 256
SUBLANES = 8


def _params(*semantics):
    return pltpu.CompilerParams(dimension_semantics=semantics, vmem_limit_bytes=VMEM_LIMIT_BYTES)


def _rms_bf16(x, g):
    ms = jnp.mean(x * x, axis=-1, keepdims=True)
    return (x * lax.rsqrt(ms + NORM_EPS) * g).astype(BF16)


def _dot(a, b):
    return jnp.dot(a, b, preferred_element_type=F32)


def _dot_nt(a, b):
    return lax.dot_general(a, b, (((1,), (1,)), ((), ())), preferred_element_type=F32)


def _rot_cols(w):
    half = w.shape[-1] // 2
    return jnp.concatenate([-w[..., half:], w[..., :half]], axis=-1)


def _swap_halves(g):
    half = g.shape[-1] // 2
    return jnp.concatenate([g[..., half:], g[..., :half]], axis=-1)


def _prep_w_in_body(wt_ref, main_ref, lat_ref, kk_ref):
    def put(dst_ref, src0, n_cols):
        for c in range(0, n_cols, PREP_CHUNK):
            dst_ref[:, c:c + PREP_CHUNK] = wt_ref[src0 + c:src0 + c + PREP_CHUNK, :].T.astype(BF16)

    put(main_ref, OFF_CONV, MAIN_COLS)
    put(lat_ref, OFF_CQ, MLA_LATENT)
    v = wt_ref[OFF_KR:OFF_KR + 2 * MLA_ROPE, :].T
    lane = lax.broadcasted_iota(jnp.int32, v.shape, 1)
    half = MLA_ROPE // 2
    kk = jnp.where(lane < MLA_ROPE, v,
                   jnp.where(lane < MLA_ROPE + half, -pltpu.roll(v, half, 1), pltpu.roll(v, MLA_ROPE + half, 1)))
    kk_ref[...] = kk.astype(BF16)


def _prep_w_in(w_in):
    d_in = w_in.shape[-1]
    wt = jnp.swapaxes(w_in, 1, 2)
    spec = lambda n: pl.BlockSpec((None, TR_PREP, n), lambda l, i: (l, i, 0))
    return pl.pallas_call(
        _prep_w_in_body,
        grid=(DEPTH, D_MODEL // TR_PREP),
        in_specs=[pl.BlockSpec((None, d_in, TR_PREP), lambda l, i: (l, 0, i))],
        out_specs=[spec(MAIN_COLS), spec(MLA_LATENT), spec(2 * MLA_ROPE)],
        out_shape=[jax.ShapeDtypeStruct((DEPTH, D_MODEL, MAIN_COLS), BF16),
                   jax.ShapeDtypeStruct((DEPTH, D_MODEL, MLA_LATENT), BF16),
                   jax.ShapeDtypeStruct((DEPTH, D_MODEL, 2 * MLA_ROPE), BF16)],
        compiler_params=_params("parallel", "parallel"),
        name="prep_w_in",
    )(wt)


def _ffn_body(x_ref, g_ref, wg_ref, wu_ref, wd_ref, o_ref, h_scr):
    j = pl.program_id(1)

    @pl.when(j == 0)
    def _():
        h_scr[...] = _rms_bf16(x_ref[...], g_ref[...])
        o_ref[...] = jnp.zeros_like(o_ref)

    h = h_scr[...]
    a = _dot(h, wg_ref[...].astype(BF16))
    b = _dot(h, wu_ref[...].astype(BF16))
    t = (a * jax.nn.sigmoid(a) * b).astype(BF16)
    o_ref[...] += _dot(t, wd_ref[...].astype(BF16))

    @pl.when(j == pl.num_programs(1) - 1)
    def _():
        o_ref[...] = x_ref[...] + 0.5 * o_ref[...]


def _ffn(x, g, wg, wu, wd, l):
    tm, tf = TM_FFN, TF_FFN
    return pl.pallas_call(
        _ffn_body,
        grid=(TOKENS // tm, D_FF // tf),
        in_specs=[
            pl.BlockSpec((tm, D_MODEL), lambda i, j: (i, 0)),
            pl.BlockSpec((1, D_MODEL), lambda i, j: (0, 0)),
            pl.BlockSpec((None, D_MODEL, tf), lambda i, j: (l, 0, j)),
            pl.BlockSpec((None, D_MODEL, tf), lambda i, j: (l, 0, j)),
            pl.BlockSpec((None, tf, D_MODEL), lambda i, j: (l, j, 0)),
        ],
        out_specs=pl.BlockSpec((tm, D_MODEL), lambda i, j: (i, 0)),
        out_shape=jax.ShapeDtypeStruct((TOKENS, D_MODEL), F32),
        scratch_shapes=[pltpu.VMEM((tm, D_MODEL), BF16)],
        compiler_params=_params("parallel", "arbitrary"),
        name="ffn",
    )(x, g, wg, wu, wd)


def _rope_pair(t, table):
    u = t * table
    return (u + pltpu.roll(u, MLA_ROPE, 1))[:, :MLA_ROPE]


def _mla_proj_body(x_ref, gmix_ref, wc_ref, wkk_ref, gcq_ref, gckv_ref, wq_ref, wkv_ref,
                   gqn_ref, g2q_ref, gkn_ref, g2k_ref, cs_ref, q_ref, k_ref, v_ref):
    for r0 in range(0, x_ref.shape[0], MLA_PROJ_ROWS):
        _mla_proj_rows(slice(r0, r0 + MLA_PROJ_ROWS), x_ref, gmix_ref, wc_ref, wkk_ref, gcq_ref, gckv_ref,
                       wq_ref, wkv_ref, gqn_ref, g2q_ref, gkn_ref, g2k_ref, cs_ref, q_ref, k_ref, v_ref)


def _mla_proj_rows(rows, x_ref, gmix_ref, wc_ref, wkk_ref, gcq_ref, gckv_ref, wq_ref, wkv_ref,
                   gqn_ref, g2q_ref, gkn_ref, g2k_ref, cs_ref, q_ref, k_ref, v_ref):
    hn = _rms_bf16(x_ref[rows, :], gmix_ref[...])
    c = _dot(hn, wc_ref[...])
    cqn = _rms_bf16(c[:, :MLA_Q_RANK], gcq_ref[...])
    ckvn = _rms_bf16(c[:, MLA_Q_RANK:], gckv_ref[...])
    kk = _dot(hn, wkk_ref[...])
    low = lax.broadcasted_iota(jnp.int32, kk.shape, 1) < MLA_ROPE
    cs = cs_ref[rows, :]
    kr_ss = jnp.sum(jnp.where(low, kk * kk, 0.0), axis=-1, keepdims=True)
    kr_rot = _rope_pair(kk, cs * g2k_ref[...])
    cs_q = cs * g2q_ref[...]
    q_scale = MLA_QK ** -0.5 * LOG2_E
    hw = MLA_NOPE + 2 * MLA_ROPE
    for h in range(MLA_HEADS):
        rq = _dot(cqn, wq_ref[:, h * hw:(h + 1) * hw])
        qn, qt = rq[:, :MLA_NOPE], rq[:, MLA_NOPE:]
        ss = (jnp.sum(qn * qn, axis=-1, keepdims=True)
              + jnp.sum(jnp.where(low, qt * qt, 0.0), axis=-1, keepdims=True)) / MLA_QK
        r = lax.rsqrt(ss + NORM_EPS) * q_scale
        q_ref[0, h, rows, :MLA_NOPE] = (qn * r * gqn_ref[...]).astype(BF16)
        q_ref[0, h, rows, MLA_NOPE:] = (_rope_pair(qt, cs_q) * r).astype(BF16)
        rkv = _dot(ckvn, wkv_ref[:, h * hw:(h + 1) * hw])
        kn = rkv[:, :MLA_NOPE]
        ss = (jnp.sum(kn * kn, axis=-1, keepdims=True) + kr_ss) / MLA_QK
        r = lax.rsqrt(ss + NORM_EPS)
        k_ref[0, h, rows, :MLA_NOPE] = (kn * r * gkn_ref[...]).astype(BF16)
        k_ref[0, h, rows, MLA_NOPE:] = (kr_rot * r).astype(BF16)
        v_ref[0, h, rows, :] = rkv[:, MLA_NOPE:].astype(BF16)


def _mla_proj(x, gmix, w_lat, w_kk, l, gcq, gckv, wq, wkv, gq, gk, cs):
    tm = TM_PROJ
    n_s = SEQ // tm
    gq = gq.reshape(1, MLA_QK)
    gk = gk.reshape(1, MLA_QK)
    pair = lambda g: jnp.concatenate([g[:, MLA_NOPE:], _swap_halves(g[:, MLA_NOPE:])], axis=-1)
    vec = lambda n: pl.BlockSpec((1, n), lambda i: (0, 0))
    head_spec = lambda d: pl.BlockSpec((1, MLA_HEADS, tm, d), lambda i: (i // n_s, 0, i % n_s, 0))
    return pl.pallas_call(
        _mla_proj_body,
        grid=(TOKENS // tm,),
        in_specs=[
            pl.BlockSpec((tm, D_MODEL), lambda i: (i, 0)),
            vec(D_MODEL),
            pl.BlockSpec((None, D_MODEL, MLA_LATENT), lambda i: (l, 0, 0)),
            pl.BlockSpec((None, D_MODEL, 2 * MLA_ROPE), lambda i: (l, 0, 0)),
            vec(MLA_Q_RANK), vec(MLA_KV_RANK),
            pl.BlockSpec((None,) + wq.shape[1:], lambda i: (l, 0, 0)),
            pl.BlockSpec((None,) + wkv.shape[1:], lambda i: (l, 0, 0)),
            vec(MLA_NOPE), vec(2 * MLA_ROPE), vec(MLA_NOPE), vec(2 * MLA_ROPE),
            pl.BlockSpec((tm, 2 * MLA_ROPE), lambda i: (i % n_s, 0)),
        ],
        out_specs=[head_spec(MLA_QK), head_spec(MLA_QK), head_spec(MLA_V)],
        out_shape=[jax.ShapeDtypeStruct((BATCH, MLA_HEADS, SEQ, MLA_QK), BF16),
                   jax.ShapeDtypeStruct((BATCH, MLA_HEADS, SEQ, MLA_QK), BF16),
                   jax.ShapeDtypeStruct((BATCH, MLA_HEADS, SEQ, MLA_V), BF16)],
        compiler_params=_params("parallel"),
        name="mla_proj",
    )(x, gmix, w_lat, w_kk, gcq.reshape(1, -1), gckv.reshape(1, -1), wq, wkv,
      gq[:, :MLA_NOPE], pair(gq), gk[:, :MLA_NOPE], pair(gk), cs)


N_ATTN_TILES = SEQ // TQ_MLA
N_SCORE_CHUNKS = N_ATTN_TILES * (N_ATTN_TILES + 1) // 2


def _attend_tile(qi, q, k_chunk, v_chunk, s_scr, bias_fn):
    t = TQ_MLA
    base = qi * (qi + 1) // 2
    row = lax.broadcasted_iota(jnp.int32, (t, t), 0)
    col = lax.broadcasted_iota(jnp.int32, (t, t), 1)
    mx = None
    for j in range(qi + 1):
        s = _dot_nt(q, k_chunk(j))
        if bias_fn is not None:
            s = bias_fn(j, s)
        if j == qi:
            s = jnp.where(col <= row, s, NEG_INF)
        s_scr[:, (base + j) * t:(base + j + 1) * t] = s
        mx = s if mx is None else jnp.maximum(mx, s)
    m = jnp.max(mx, axis=-1, keepdims=True)
    lsum = None
    acc = None
    for j in range(qi + 1):
        p = jnp.exp2(s_scr[:, (base + j) * t:(base + j + 1) * t] - m)
        lsum = p if lsum is None else lsum + p
        pv = _dot(p.astype(BF16), v_chunk(j))
        acc = pv if acc is None else acc + pv
    return acc / jnp.sum(lsum, axis=-1, keepdims=True)


def _mla_attn_body(q_ref, k_ref, v_ref, o_ref, s_scr):
    t = TQ_MLA
    for qi in range(N_ATTN_TILES):
        o = _attend_tile(qi, q_ref[0, 0, qi * t:(qi + 1) * t, :],
                         lambda j: k_ref[0, 0, j * t:(j + 1) * t, :],
                         lambda j: v_ref[0, 0, j * t:(j + 1) * t, :], s_scr, None)
        o_ref[0, qi * t:(qi + 1) * t, :] = o.astype(BF16)


def _mla_attn(q, k, v):
    t = TQ_MLA
    head = lambda d: pl.BlockSpec((1, 1, SEQ, d), lambda b, h: (b, h, 0, 0))
    return pl.pallas_call(
        _mla_attn_body,
        grid=(BATCH, MLA_HEADS),
        in_specs=[head(MLA_QK), head(MLA_QK), head(MLA_V)],
        out_specs=pl.BlockSpec((1, SEQ, MLA_V), lambda b, h: (b, 0, h)),
        out_shape=jax.ShapeDtypeStruct((BATCH, SEQ, MLA_HEADS * MLA_V), BF16),
        scratch_shapes=[pltpu.VMEM((t, N_SCORE_CHUNKS * t), F32)],
        compiler_params=_params("parallel", "parallel"),
        name="mla_attn",
    )(q, k, v)


def _glu_proj_body(x_ref, gmix_ref, wa_ref, wg_ref, z_ref):
    for r0 in range(0, x_ref.shape[0], PROJ_ROWS):
        rows = slice(r0, r0 + PROJ_ROWS)
        h = _rms_bf16(x_ref[rows, :], gmix_ref[...])
        a = _dot(h, wa_ref[...])
        g = _dot(h, wg_ref[...])
        z_ref[rows, :] = a * jax.nn.sigmoid(g)


def _glu_proj(x, gmix, w_in_b, l):
    tm, tn = TM_PROJ, CONV_CH
    return pl.pallas_call(
        _glu_proj_body,
        grid=(TOKENS // tm,),
        in_specs=[
            pl.BlockSpec((tm, D_MODEL), lambda i: (i, 0)),
            pl.BlockSpec((1, D_MODEL), lambda i: (0, 0)),
            pl.BlockSpec((None, D_MODEL, tn), lambda i: (l, 0, COL_CONV_A // tn)),
            pl.BlockSpec((None, D_MODEL, tn), lambda i: (l, 0, COL_CONV_G // tn)),
        ],
        out_specs=pl.BlockSpec((tm, tn), lambda i: (i, 0)),
        out_shape=jax.ShapeDtypeStruct((TOKENS, CONV_CH), F32),
        compiler_params=_params("parallel"),
        name="glu_proj",
    )(x, gmix, w_in_b, w_in_b)


def _conv_body(z_ref, w_ref, b_ref, lg_ref, lb_ref, o_ref, zs):
    ts, halo = TS_CONV, CONV_HALO
    si = pl.program_id(1)

    @pl.when(si == 0)
    def _():
        zs[0, 0:halo, :] = jnp.zeros((halo, CONV_CH), F32)

    @pl.when(si > 0)
    def _():
        zs[0, 0:halo, :] = zs[0, ts:ts + halo, :]

    zs[0, halo:halo + ts, :] = z_ref[...]
    n_shifted = ts + halo - SUBLANES
    for s in range(1, SUBLANES):
        zs[s, 0:n_shifted, :] = zs[0, s:s + n_shifted, :]
    shift = halo - (CONV_WIDTH - 1)
    for r0 in range(0, ts, CONV_ROWS):
        parts = []
        for c0 in range(0, CONV_CH, CONV_LANES):
            acc = jnp.zeros((CONV_ROWS, CONV_LANES), F32) + b_ref[:, c0:c0 + CONV_LANES]
            for k in range(CONV_WIDTH):
                s = (shift + k) % SUBLANES
                a = r0 + shift + k - s
                acc = acc + zs[s, a:a + CONV_ROWS, c0:c0 + CONV_LANES] * w_ref[k:k + 1, c0:c0 + CONV_LANES]
            parts.append(acc)
        acc = jnp.concatenate(parts, axis=-1)
        mu = jnp.mean(acc, axis=-1, keepdims=True)
        xc = acc - mu
        var = jnp.mean(xc * xc, axis=-1, keepdims=True)
        y = xc * lax.rsqrt(var + NORM_EPS) * lg_ref[...] + lb_ref[...]
        o_ref[r0:r0 + CONV_ROWS, :] = (y * jax.nn.sigmoid(y)).astype(BF16)


def _conv(z, w_dw, b_dw, ln_g, ln_b):
    ts = TS_CONV
    n_s = SEQ // ts
    vec = pl.BlockSpec((1, CONV_CH), lambda b, s: (0, 0))
    return pl.pallas_call(
        _conv_body,
        grid=(BATCH, n_s),
        in_specs=[
            pl.BlockSpec((ts, CONV_CH), lambda b, s: (b * n_s + s, 0)),
            pl.BlockSpec((CONV_WIDTH, CONV_CH), lambda b, s: (0, 0)),
            vec, vec, vec,
        ],
        out_specs=pl.BlockSpec((ts, CONV_CH), lambda b, s: (b * n_s + s, 0)),
        out_shape=jax.ShapeDtypeStruct((TOKENS, CONV_CH), BF16),
        scratch_shapes=[pltpu.VMEM((SUBLANES, ts + CONV_HALO, CONV_CH), F32)],
        compiler_params=_params("arbitrary", "arbitrary"),
        name="conv",
    )(z, w_dw, b_dw.reshape(1, -1), ln_g.reshape(1, -1), ln_b.reshape(1, -1))


def _moba_proj_body(x_ref, gmix_ref, wq_ref, wk_ref, wv_ref, g_ref, o_ref):
    for r0 in range(0, x_ref.shape[0], PROJ_ROWS):
        rows = slice(r0, r0 + PROJ_ROWS)
        hn = _rms_bf16(x_ref[rows, :], gmix_ref[...])
        for part, w_ref in enumerate((wq_ref, wk_ref, wv_ref)):
            u = _dot(hn, w_ref[...])
            for h in range(MOBA_HEADS):
                uh = u[:, h * MOBA_HEAD_DIM:(h + 1) * MOBA_HEAD_DIM]
                if part < 2:
                    ms = jnp.mean(uh * uh, axis=-1, keepdims=True)
                    uh = uh * lax.rsqrt(ms + NORM_EPS) * g_ref[part]
                o_ref[part, 0, h, rows, :] = uh


def _moba_proj(x, gmix, w_in_b, l, gq, gk):
    tm = TM_PROJ
    n_s = SEQ // tm
    hd = MOBA_HEADS * MOBA_HEAD_DIM
    g = jnp.stack([gq, gk]).reshape(2, 1, MOBA_HEAD_DIM)
    w_spec = lambda part: pl.BlockSpec((None, D_MODEL, hd), lambda i: (l, 0, COL_MOBA // hd + part))
    return pl.pallas_call(
        _moba_proj_body,
        grid=(TOKENS // tm,),
        in_specs=[
            pl.BlockSpec((tm, D_MODEL), lambda i: (i, 0)),
            pl.BlockSpec((1, D_MODEL), lambda i: (0, 0)),
            w_spec(0), w_spec(1), w_spec(2),
            pl.BlockSpec((2, 1, MOBA_HEAD_DIM), lambda i: (0, 0, 0)),
        ],
        out_specs=pl.BlockSpec((3, 1, MOBA_HEADS, tm, MOBA_HEAD_DIM),
                               lambda i: (0, i // n_s, 0, i % n_s, 0)),
        out_shape=jax.ShapeDtypeStruct((3, BATCH, MOBA_HEADS, SEQ, MOBA_HEAD_DIM), F32),
        compiler_params=_params("parallel"),
        name="moba_proj",
    )(x, gmix, w_in_b, w_in_b, w_in_b, g)


def _moba_attn_body(slope_ref, q_ref, k_ref, v_ref, o_ref, kb_scr, vb_scr, s_scr):
    blk, nb = MOBA_BLOCK, MOBA_NB
    kb_scr[...] = k_ref[0, 0, 0].astype(BF16)
    vb_scr[...] = v_ref[0, 0, 0].astype(BF16)
    km = jnp.concatenate(
        [jnp.mean(k_ref[0, 0, 0, j * blk:(j + 1) * blk, :], axis=0, keepdims=True) for j in range(nb)], axis=0)
    slope = slope_ref[0][:, 0:1] * LOG2_E
    key_pos = lax.broadcasted_iota(jnp.int32, (1, blk), 1).astype(F32)
    lane = lax.broadcasted_iota(jnp.int32, (blk, nb), 1)

    for n in range(nb):
        q = q_ref[0, 0, 0, n * blk:(n + 1) * blk, :]
        row_bias = None
        if n > MOBA_TOPK:
            cols = []
            gate = jnp.full((blk, nb), NEG_INF, F32)
            for j in range(nb):
                if j < n:
                    gj = jnp.sum(q * km[j:j + 1, :], axis=-1, keepdims=True)
                    gate = jnp.where(lane == j, gj, gate)
                else:
                    gj = jnp.full((blk, 1), NEG_INF, F32)
                cols.append(gj)
            rank = jnp.zeros((blk, nb), jnp.int32)
            for j in range(nb):
                wins = jnp.where(cols[j] > gate, 1, 0)
                wins_or_ties = jnp.where(cols[j] >= gate, 1, 0)
                rank = rank + jnp.where(lane > j, wins_or_ties, wins)
            row_bias = jnp.where((rank < MOBA_TOPK) & (lane < n), 0.0, NEG_INF)

        def bias_fn(j, s, n=n, row_bias=row_bias):
            s = s + slope * (key_pos + float(j * blk))
            if row_bias is not None and j < n:
                s = s + row_bias[:, j:j + 1]
            return s

        qb = (q * (MOBA_HEAD_DIM ** -0.5 * LOG2_E)).astype(BF16)
        o = _attend_tile(n, qb, lambda j: kb_scr[j * blk:(j + 1) * blk, :],
                         lambda j: vb_scr[j * blk:(j + 1) * blk, :], s_scr, bias_fn)
        o_ref[0, n * blk:(n + 1) * blk, :] = o.astype(BF16)


def _moba_attn(qkv, slopes):
    blk = MOBA_BLOCK
    assert blk == TQ_MLA
    spec = lambda which: pl.BlockSpec((1, 1, 1, SEQ, MOBA_HEAD_DIM), lambda b, h: (which, b, h, 0, 0))
    return pl.pallas_call(
        _moba_attn_body,
        grid=(BATCH, MOBA_HEADS),
        in_specs=[pl.BlockSpec((1, 1, 128), lambda b, h: (h, 0, 0)), spec(0), spec(1), spec(2)],
        out_specs=pl.BlockSpec((1, SEQ, MOBA_HEAD_DIM), lambda b, h: (b, 0, h)),
        out_shape=jax.ShapeDtypeStruct((BATCH, SEQ, MOBA_HEADS * MOBA_HEAD_DIM), BF16),
        scratch_shapes=[pltpu.VMEM((SEQ, MOBA_HEAD_DIM), BF16), pltpu.VMEM((SEQ, MOBA_HEAD_DIM), BF16),
                        pltpu.VMEM((blk, N_SCORE_CHUNKS * blk), F32)],
        compiler_params=_params("parallel", "parallel"),
        name="moba_attn",
    )(slopes, qkv, qkv, qkv)


def _merge_body(x_ref, gmix_ref, oa_ref, ob_ref, oc_ref, wga_ref, wgb_ref, wgc_ref, bga_ref, bgb_ref, bgc_ref,
                woa_ref, wob_ref, woc_ref, wout_ref, o_ref, h_scr, acc_scr):
    j = pl.program_id(1)

    @pl.when(j == 0)
    def _():
        h_scr[...] = _rms_bf16(x_ref[...], gmix_ref[...])
        acc_scr[...] = jnp.zeros_like(acc_scr)

    h = h_scr[...]
    merged = None
    for o_i, wg_i, bg_i, wo_i in ((oa_ref, wga_ref, bga_ref, woa_ref), (ob_ref, wgb_ref, bgb_ref, wob_ref),
                                  (oc_ref, wgc_ref, bgc_ref, woc_ref)):
        gate = jax.nn.sigmoid(_dot(h, wg_i[...]) + bg_i[...])
        term = gate * _dot(o_i[...], wo_i[...])
        merged = term if merged is None else merged + term
    acc_scr[...] += _dot(merged.astype(BF16), wout_ref[...])

    @pl.when(j == pl.num_programs(1) - 1)
    def _():
        o_ref[...] = x_ref[...] + acc_scr[...]


def _merge(x, gmix, oa, ob, oc, w_in_b, b_gate, w_oa, w_ob, w_oc, w_out, l):
    tm, tn = TM_MERGE, TN_MERGE
    n_j = D_MODEL // tn
    tok = lambda d: pl.BlockSpec((tm, d), lambda i, j: (i, 0))
    wg = lambda b: pl.BlockSpec((None, D_MODEL, tn), lambda i, j: (l, 0, COL_GATE // tn + b * n_j + j))
    bg = lambda b: pl.BlockSpec((None, 1, tn), lambda i, j: (l, 0, b * n_j + j))
    wo = lambda d: pl.BlockSpec((None, d, tn), lambda i, j: (l, 0, j))
    return pl.pallas_call(
        _merge_body,
        grid=(TOKENS // tm, n_j),
        in_specs=[
            tok(D_MODEL),
            pl.BlockSpec((1, D_MODEL), lambda i, j: (0, 0)),
            tok(MLA_HEADS * MLA_V), tok(CONV_CH), tok(MOBA_HEADS * MOBA_HEAD_DIM),
            wg(0), wg(1), wg(2), bg(0), bg(1), bg(2),
            wo(MLA_HEADS * MLA_V), wo(CONV_CH), wo(MOBA_HEADS * MOBA_HEAD_DIM),
            pl.BlockSpec((None, tn, D_MODEL), lambda i, j: (l, j, 0)),
        ],
        out_specs=tok(D_MODEL),
        out_shape=jax.ShapeDtypeStruct((TOKENS, D_MODEL), F32),
        scratch_shapes=[pltpu.VMEM((tm, D_MODEL), BF16), pltpu.VMEM((tm, D_MODEL), F32)],
        compiler_params=_params("parallel", "arbitrary"),
        name="merge",
    )(x, gmix, oa, ob, oc, w_in_b, w_in_b, w_in_b, b_gate, b_gate, b_gate, w_oa, w_ob, w_oc, w_out)


def _rope_table():
    half = MLA_ROPE // 2
    inv_freq = jnp.exp(-math.log(ROPE_THETA) * jnp.arange(half, dtype=F32) * 2.0 / MLA_ROPE)
    ang = jnp.arange(SEQ, dtype=jnp.int32).astype(F32)[:, None] * inv_freq[None, :]
    cos, sin = jnp.cos(ang), jnp.sin(ang)
    return jnp.concatenate([cos, cos, sin, sin], axis=-1)


def kernel(x, ffn1_norm, ffn1_w_gate, ffn1_w_up, ffn1_w_down, mix_norm, w_in, b_gate, mla_cq_norm, mla_ckv_norm, mla_w_uq, mla_w_ukv, mla_q_norm, mla_k_norm, mla_w_o, conv_w_dw, conv_b_dw, conv_ln_g, conv_ln_b, conv_w_pw, moba_q_norm, moba_k_norm, moba_w_o, w_out, ffn2_norm, ffn2_w_gate, ffn2_w_up, ffn2_w_down):
    cs = _rope_table()
    slopes = jnp.exp2(-8.0 * jnp.arange(1, MOBA_HEADS + 1, dtype=F32) / MOBA_HEADS)
    slopes = jnp.broadcast_to(slopes[:, None, None], (MOBA_HEADS, 1, 128))
    ffn1 = (ffn1_w_gate, ffn1_w_up, ffn1_w_down)
    ffn2 = (ffn2_w_gate, ffn2_w_up, ffn2_w_down)
    w_in_b, w_lat, w_kk = _prep_w_in(w_in)
    wq = mla_w_uq.reshape(DEPTH, MLA_Q_RANK, MLA_HEADS, MLA_QK)
    wq = jnp.concatenate([wq, _rot_cols(wq[..., MLA_NOPE:])], axis=-1).astype(BF16)
    wq = wq.reshape(DEPTH, MLA_Q_RANK, MLA_HEADS * (MLA_NOPE + 2 * MLA_ROPE))
    wkv = mla_w_ukv.astype(BF16)
    w_oa, w_ob, w_oc, w_o = (w.astype(BF16) for w in (mla_w_o, conv_w_pw, moba_w_o, w_out))
    bg = b_gate.reshape(DEPTH, 1, N_BRANCH * D_MODEL)
    xt = x.reshape(TOKENS, D_MODEL)
    for l in range(DEPTH):
        xt = _ffn(xt, ffn1_norm[l].reshape(1, -1), *ffn1, l)
        gmix = mix_norm[l].reshape(1, -1)
        q, k, v = _mla_proj(xt, gmix, w_lat, w_kk, l, mla_cq_norm[l], mla_ckv_norm[l], wq, wkv,
                            mla_q_norm[l], mla_k_norm[l], cs)
        o_mla = _mla_attn(q, k, v).reshape(TOKENS, MLA_HEADS * MLA_V)
        z = _glu_proj(xt, gmix, w_in_b, l)
        o_conv = _conv(z, conv_w_dw[l], conv_b_dw[l], conv_ln_g[l], conv_ln_b[l])
        qkv = _moba_proj(xt, gmix, w_in_b, l, moba_q_norm[l], moba_k_norm[l])
        o_moba = _moba_attn(qkv, slopes).reshape(TOKENS, MOBA_HEADS * MOBA_HEAD_DIM)
        xt = _merge(xt, gmix, o_mla, o_conv, o_moba, w_in_b, bg, w_oa, w_ob, w_oc, w_o, l)
        xt = _ffn(xt, ffn2_norm[l].reshape(1, -1), *ffn2, l)
    return xt.reshape(BATCH, SEQ, D_MODEL)
```

```python
import math

import jax
import jax.numpy as jnp
from jax import lax
from jax.experimental import pallas as pl
from jax.experimental.pallas import tpu as pltpu

D_MODEL = 2048
BATCH = 4
SEQ = 2048
DEPTH = 2
TOKENS = BATCH * SEQ

MLA_HEADS = 8
MLA_Q_RANK = 768
MLA_KV_RANK = 512
MLA_NOPE = 128
MLA_ROPE = 64
MLA_QK = MLA_NOPE + MLA_ROPE
MLA_V = 128
ROPE_THETA = 10000.0

CONV_CH = 1024
CONV_WIDTH = 31
CONV_HALO = 32

MOBA_HEADS = 8
MOBA_HEAD_DIM = 128
MOBA_BLOCK = 256
MOBA_TOPK = 3
MOBA_NB = SEQ // MOBA_BLOCK

D_FF = 5632
N_BRANCH = 3
NORM_EPS = 1e-6
NEG_INF = -1e30
LOG2_E = math.log2(math.e)

OFF_CQ = 0
OFF_CKV = OFF_CQ + MLA_Q_RANK
OFF_KR = OFF_CKV + MLA_KV_RANK
OFF_CONV = OFF_KR + MLA_ROPE
OFF_MOBA = OFF_CONV + 2 * CONV_CH
OFF_GATE = OFF_MOBA + 3 * MOBA_HEADS * MOBA_HEAD_DIM

MLA_LATENT = MLA_Q_RANK + MLA_KV_RANK
MAIN_COLS = N_BRANCH * D_MODEL + OFF_GATE - OFF_CONV
COL_CONV_A = 0
COL_CONV_G = COL_CONV_A + CONV_CH
COL_MOBA = OFF_MOBA - OFF_CONV
COL_GATE = OFF_GATE - OFF_CONV
TR_PREP = 256
PREP_CHUNK = 256

F32 = jnp.float32
BF16 = jnp.bfloat16

VMEM_LIMIT_BYTES = 56 * 1024 * 1024

TM_FFN = 1024
TF_FFN = 256
TM_PROJ = 512
MLA_PROJ_ROWS = 128
PROJ_ROWS = 256
TM_MERGE = 512
TN_MERGE = 512
TQ_MLA = 256
TS_CONV = 256
CONV_ROWS = 64
CONV_LANES = 256
SUBLANES = 8


def _params(*semantics):
    return pltpu.CompilerParams(dimension_semantics=semantics, vmem_limit_bytes=VMEM_LIMIT_BYTES)


def _rms_bf16(x, g):
    ms = jnp.mean(x * x, axis=-1, keepdims=True)
    return (x * lax.rsqrt(ms + NORM_EPS) * g).astype(BF16)


def _dot(a, b):
    return jnp.dot(a, b, preferred_element_type=F32)


def _dot_nt(a, b):
    return lax.dot_general(a, b, (((1,), (1,)), ((), ())), preferred_element_type=F32)


def _rot_cols(w):
    half = w.shape[-1] // 2
    return jnp.concatenate([-w[..., half:], w[..., :half]], axis=-1)


def _swap_halves(g):
    half = g.shape[-1] // 2
    return jnp.concatenate([g[..., half:], g[..., :half]], axis=-1)


def _prep_w_in_body(wt_ref, main_ref, lat_ref, kk_ref):
    def put(dst_ref, src0, n_cols):
        for c in range(0, n_cols, PREP_CHUNK):
            dst_ref[:, c:c + PREP_CHUNK] = wt_ref[src0 + c:src0 + c + PREP_CHUNK, :].T.astype(BF16)

    put(main_ref, OFF_CONV, MAIN_COLS)
    put(lat_ref, OFF_CQ, MLA_LATENT)
    v = wt_ref[OFF_KR:OFF_KR + 2 * MLA_ROPE, :].T
    lane = lax.broadcasted_iota(jnp.int32, v.shape, 1)
    half = MLA_ROPE // 2
    kk = jnp.where(lane < MLA_ROPE, v,
                   jnp.where(lane < MLA_ROPE + half, -pltpu.roll(v, half, 1), pltpu.roll(v, MLA_ROPE + half, 1)))
    kk_ref[...] = kk.astype(BF16)


def _prep_w_in(w_in):
    d_in = w_in.shape[-1]
    wt = jnp.swapaxes(w_in, 1, 2)
    spec = lambda n: pl.BlockSpec((None, TR_PREP, n), lambda l, i: (l, i, 0))
    return pl.pallas_call(
        _prep_w_in_body,
        grid=(DEPTH, D_MODEL // TR_PREP),
        in_specs=[pl.BlockSpec((None, d_in, TR_PREP), lambda l, i: (l, 0, i))],
        out_specs=[spec(MAIN_COLS), spec(MLA_LATENT), spec(2 * MLA_ROPE)],
        out_shape=[jax.ShapeDtypeStruct((DEPTH, D_MODEL, MAIN_COLS), BF16),
                   jax.ShapeDtypeStruct((DEPTH, D_MODEL, MLA_LATENT), BF16),
                   jax.ShapeDtypeStruct((DEPTH, D_MODEL, 2 * MLA_ROPE), BF16)],
        compiler_params=_params("parallel", "parallel"),
        name="prep_w_in",
    )(wt)


def _ffn_body(x_ref, g_ref, wg_ref, wu_ref, wd_ref, o_ref, h_scr):
    j = pl.program_id(1)

    @pl.when(j == 0)
    def _():
        h_scr[...] = _rms_bf16(x_ref[...], g_ref[...])
        o_ref[...] = jnp.zeros_like(o_ref)

    h = h_scr[...]
    a = _dot(h, wg_ref[...].astype(BF16))
    b = _dot(h, wu_ref[...].astype(BF16))
    t = (a * jax.nn.sigmoid(a) * b).astype(BF16)
    o_ref[...] += _dot(t, wd_ref[...].astype(BF16))

    @pl.when(j == pl.num_programs(1) - 1)
    def _():
        o_ref[...] = x_ref[...] + 0.5 * o_ref[...]


def _ffn(x, g, wg, wu, wd, l):
    tm, tf = TM_FFN, TF_FFN
    return pl.pallas_call(
        _ffn_body,
        grid=(TOKENS // tm, D_FF // tf),
        in_specs=[
            pl.BlockSpec((tm, D_MODEL), lambda i, j: (i, 0)),
            pl.BlockSpec((None, 1, D_MODEL), lambda i, j: (l, 0, 0)),
            pl.BlockSpec((None, D_MODEL, tf), lambda i, j: (l, 0, j)),
            pl.BlockSpec((None, D_MODEL, tf), lambda i, j: (l, 0, j)),
            pl.BlockSpec((None, tf, D_MODEL), lambda i, j: (l, j, 0)),
        ],
        out_specs=pl.BlockSpec((tm, D_MODEL), lambda i, j: (i, 0)),
        out_shape=jax.ShapeDtypeStruct((TOKENS, D_MODEL), F32),
        scratch_shapes=[pltpu.VMEM((tm, D_MODEL), BF16)],
        compiler_params=_params("parallel", "arbitrary"),
        name="ffn",
    )(x, g, wg, wu, wd)


def _rope_pair(t, table):
    u = t * table
    return (u + pltpu.roll(u, MLA_ROPE, 1))[:, :MLA_ROPE]


def _mla_proj_body(x_ref, gmix_ref, wc_ref, wkk_ref, gcq_ref, gckv_ref, wq_ref, wkv_ref,
                   gqn_ref, g2q_ref, gkn_ref, g2k_ref, cs_ref, q_ref, k_ref, v_ref):
    for r0 in range(0, x_ref.shape[0], MLA_PROJ_ROWS):
        _mla_proj_rows(slice(r0, r0 + MLA_PROJ_ROWS), x_ref, gmix_ref, wc_ref, wkk_ref, gcq_ref, gckv_ref,
                       wq_ref, wkv_ref, gqn_ref, g2q_ref, gkn_ref, g2k_ref, cs_ref, q_ref, k_ref, v_ref)


def _mla_proj_rows(rows, x_ref, gmix_ref, wc_ref, wkk_ref, gcq_ref, gckv_ref, wq_ref, wkv_ref,
                   gqn_ref, g2q_ref, gkn_ref, g2k_ref, cs_ref, q_ref, k_ref, v_ref):
    hn = _rms_bf16(x_ref[rows, :], gmix_ref[...])
    c = _dot(hn, wc_ref[...])
    cqn = _rms_bf16(c[:, :MLA_Q_RANK], gcq_ref[...])
    ckvn = _rms_bf16(c[:, MLA_Q_RANK:], gckv_ref[...])
    kk = _dot(hn, wkk_ref[...])
    low = lax.broadcasted_iota(jnp.int32, kk.shape, 1) < MLA_ROPE
    cs = cs_ref[rows, :]
    kr_ss = jnp.sum(jnp.where(low, kk * kk, 0.0), axis=-1, keepdims=True)
    kr_rot = _rope_pair(kk, cs * g2k_ref[...])
    cs_q = cs * g2q_ref[...]
    q_scale = MLA_QK ** -0.5 * LOG2_E
    hw = MLA_NOPE + 2 * MLA_ROPE
    for h in range(MLA_HEADS):
        rq = _dot(cqn, wq_ref[:, h * hw:(h + 1) * hw])
        qn, qt = rq[:, :MLA_NOPE], rq[:, MLA_NOPE:]
        ss = (jnp.sum(qn * qn, axis=-1, keepdims=True)
              + jnp.sum(jnp.where(low, qt * qt, 0.0), axis=-1, keepdims=True)) / MLA_QK
        r = lax.rsqrt(ss + NORM_EPS) * q_scale
        q_ref[0, h, rows, :MLA_NOPE] = (qn * r * gqn_ref[...]).astype(BF16)
        q_ref[0, h, rows, MLA_NOPE:] = (_rope_pair(qt, cs_q) * r).astype(BF16)
        rkv = _dot(ckvn, wkv_ref[:, h * hw:(h + 1) * hw])
        kn = rkv[:, :MLA_NOPE]
        ss = (jnp.sum(kn * kn, axis=-1, keepdims=True) + kr_ss) / MLA_QK
        r = lax.rsqrt(ss + NORM_EPS)
        k_ref[0, h, rows, :MLA_NOPE] = (kn * r * gkn_ref[...]).astype(BF16)
        k_ref[0, h, rows, MLA_NOPE:] = (kr_rot * r).astype(BF16)
        v_ref[0, h, rows, :] = rkv[:, MLA_NOPE:].astype(BF16)


def _mla_proj(x, gmix, w_lat, w_kk, l, gcq, gckv, wq, wkv, gq, gk, cs):
    tm = TM_PROJ
    n_s = SEQ // tm
    pair = lambda g: jnp.concatenate([g[..., MLA_NOPE:], _swap_halves(g[..., MLA_NOPE:])], axis=-1)
    vec = lambda n: pl.BlockSpec((None, 1, n), lambda i: (l, 0, 0))
    head_spec = lambda d: pl.BlockSpec((1, MLA_HEADS, tm, d), lambda i: (i // n_s, 0, i % n_s, 0))
    return pl.pallas_call(
        _mla_proj_body,
        grid=(TOKENS // tm,),
        in_specs=[
            pl.BlockSpec((tm, D_MODEL), lambda i: (i, 0)),
            vec(D_MODEL),
            pl.BlockSpec((None, D_MODEL, MLA_LATENT), lambda i: (l, 0, 0)),
            pl.BlockSpec((None, D_MODEL, 2 * MLA_ROPE), lambda i: (l, 0, 0)),
            vec(MLA_Q_RANK), vec(MLA_KV_RANK),
            pl.BlockSpec((None,) + wq.shape[1:], lambda i: (l, 0, 0)),
            pl.BlockSpec((None,) + wkv.shape[1:], lambda i: (l, 0, 0)),
            vec(MLA_NOPE), vec(2 * MLA_ROPE), vec(MLA_NOPE), vec(2 * MLA_ROPE),
            pl.BlockSpec((tm, 2 * MLA_ROPE), lambda i: (i % n_s, 0)),
        ],
        out_specs=[head_spec(MLA_QK), head_spec(MLA_QK), head_spec(MLA_V)],
        out_shape=[jax.ShapeDtypeStruct((BATCH, MLA_HEADS, SEQ, MLA_QK), BF16),
                   jax.ShapeDtypeStruct((BATCH, MLA_HEADS, SEQ, MLA_QK), BF16),
                   jax.ShapeDtypeStruct((BATCH, MLA_HEADS, SEQ, MLA_V), BF16)],
        compiler_params=_params("parallel"),
        name="mla_proj",
    )(x, gmix, w_lat, w_kk, gcq, gckv, wq, wkv,
      gq[..., :MLA_NOPE], pair(gq), gk[..., :MLA_NOPE], pair(gk), cs)


N_ATTN_TILES = SEQ // TQ_MLA
N_SCORE_CHUNKS = N_ATTN_TILES * (N_ATTN_TILES + 1) // 2


def _attend_tile(qi, q, k_chunk, v_chunk, s_scr, bias_fn):
    t = TQ_MLA
    base = qi * (qi + 1) // 2
    row = lax.broadcasted_iota(jnp.int32, (t, t), 0)
    col = lax.broadcasted_iota(jnp.int32, (t, t), 1)
    mx = None
    for j in range(qi + 1):
        s = _dot_nt(q, k_chunk(j))
        if bias_fn is not None:
            s = bias_fn(j, s)
        if j == qi:
            s = jnp.where(col <= row, s, NEG_INF)
        s_scr[:, (base + j) * t:(base + j + 1) * t] = s
        mx = s if mx is None else jnp.maximum(mx, s)
    m = jnp.max(mx, axis=-1, keepdims=True)
    lsum = None
    acc = None
    for j in range(qi + 1):
        p = jnp.exp2(s_scr[:, (base + j) * t:(base + j + 1) * t] - m)
        lsum = p if lsum is None else lsum + p
        pv = _dot(p.astype(BF16), v_chunk(j))
        acc = pv if acc is None else acc + pv
    return acc / jnp.sum(lsum, axis=-1, keepdims=True)


def _mla_attn_body(q_ref, k_ref, v_ref, o_ref, s_scr):
    t = TQ_MLA
    for qi in range(N_ATTN_TILES):
        o = _attend_tile(qi, q_ref[0, 0, qi * t:(qi + 1) * t, :],
                         lambda j: k_ref[0, 0, j * t:(j + 1) * t, :],
                         lambda j: v_ref[0, 0, j * t:(j + 1) * t, :], s_scr, None)
        o_ref[0, qi * t:(qi + 1) * t, :] = o.astype(BF16)


def _mla_attn(q, k, v):
    t = TQ_MLA
    head = lambda d: pl.BlockSpec((1, 1, SEQ, d), lambda b, h: (b, h, 0, 0))
    return pl.pallas_call(
        _mla_attn_body,
        grid=(BATCH, MLA_HEADS),
        in_specs=[head(MLA_QK), head(MLA_QK), head(MLA_V)],
        out_specs=pl.BlockSpec((1, SEQ, MLA_V), lambda b, h: (b, 0, h)),
        out_shape=jax.ShapeDtypeStruct((BATCH, SEQ, MLA_HEADS * MLA_V), BF16),
        scratch_shapes=[pltpu.VMEM((t, N_SCORE_CHUNKS * t), F32)],
        compiler_params=_params("parallel", "parallel"),
        name="mla_attn",
    )(q, k, v)


def _glu_proj_body(x_ref, gmix_ref, wa_ref, wg_ref, z_ref):
    for r0 in range(0, x_ref.shape[0], PROJ_ROWS):
        rows = slice(r0, r0 + PROJ_ROWS)
        h = _rms_bf16(x_ref[rows, :], gmix_ref[...])
        a = _dot(h, wa_ref[...])
        g = _dot(h, wg_ref[...])
        z_ref[rows, :] = a * jax.nn.sigmoid(g)


def _glu_proj(x, gmix, w_in_b, l):
    tm, tn = TM_PROJ, CONV_CH
    return pl.pallas_call(
        _glu_proj_body,
        grid=(TOKENS // tm,),
        in_specs=[
            pl.BlockSpec((tm, D_MODEL), lambda i: (i, 0)),
            pl.BlockSpec((None, 1, D_MODEL), lambda i: (l, 0, 0)),
            pl.BlockSpec((None, D_MODEL, tn), lambda i: (l, 0, COL_CONV_A // tn)),
            pl.BlockSpec((None, D_MODEL, tn), lambda i: (l, 0, COL_CONV_G // tn)),
        ],
        out_specs=pl.BlockSpec((tm, tn), lambda i: (i, 0)),
        out_shape=jax.ShapeDtypeStruct((TOKENS, CONV_CH), F32),
        compiler_params=_params("parallel"),
        name="glu_proj",
    )(x, gmix, w_in_b, w_in_b)


def _conv_body(z_ref, w_ref, b_ref, lg_ref, lb_ref, o_ref, zs):
    ts, halo = TS_CONV, CONV_HALO
    si = pl.program_id(1)

    @pl.when(si == 0)
    def _():
        zs[0, 0:halo, :] = jnp.zeros((halo, CONV_CH), F32)

    @pl.when(si > 0)
    def _():
        zs[0, 0:halo, :] = zs[0, ts:ts + halo, :]

    zs[0, halo:halo + ts, :] = z_ref[...]
    n_shifted = ts + halo - SUBLANES
    for s in range(1, SUBLANES):
        zs[s, 0:n_shifted, :] = zs[0, s:s + n_shifted, :]
    shift = halo - (CONV_WIDTH - 1)
    for r0 in range(0, ts, CONV_ROWS):
        parts = []
        for c0 in range(0, CONV_CH, CONV_LANES):
            acc = jnp.zeros((CONV_ROWS, CONV_LANES), F32) + b_ref[:, c0:c0 + CONV_LANES]
            for k in range(CONV_WIDTH):
                s = (shift + k) % SUBLANES
                a = r0 + shift + k - s
                acc = acc + zs[s, a:a + CONV_ROWS, c0:c0 + CONV_LANES] * w_ref[k:k + 1, c0:c0 + CONV_LANES]
            parts.append(acc)
        acc = jnp.concatenate(parts, axis=-1)
        mu = jnp.mean(acc, axis=-1, keepdims=True)
        xc = acc - mu
        var = jnp.mean(xc * xc, axis=-1, keepdims=True)
        y = xc * lax.rsqrt(var + NORM_EPS) * lg_ref[...] + lb_ref[...]
        o_ref[r0:r0 + CONV_ROWS, :] = (y * jax.nn.sigmoid(y)).astype(BF16)


def _conv(z, w_dw, b_dw, ln_g, ln_b, l):
    ts = TS_CONV
    n_s = SEQ // ts
    vec = pl.BlockSpec((None, 1, CONV_CH), lambda b, s: (l, 0, 0))
    return pl.pallas_call(
        _conv_body,
        grid=(BATCH, n_s),
        in_specs=[
            pl.BlockSpec((ts, CONV_CH), lambda b, s: (b * n_s + s, 0)),
            pl.BlockSpec((None, CONV_WIDTH, CONV_CH), lambda b, s: (l, 0, 0)),
            vec, vec, vec,
        ],
        out_specs=pl.BlockSpec((ts, CONV_CH), lambda b, s: (b * n_s + s, 0)),
        out_shape=jax.ShapeDtypeStruct((TOKENS, CONV_CH), BF16),
        scratch_shapes=[pltpu.VMEM((SUBLANES, ts + CONV_HALO, CONV_CH), F32)],
        compiler_params=_params("arbitrary", "arbitrary"),
        name="conv",
    )(z, w_dw, b_dw, ln_g, ln_b)


def _moba_proj_body(x_ref, gmix_ref, wq_ref, wk_ref, wv_ref, g_ref, o_ref):
    for r0 in range(0, x_ref.shape[0], PROJ_ROWS):
        rows = slice(r0, r0 + PROJ_ROWS)
        hn = _rms_bf16(x_ref[rows, :], gmix_ref[...])
        for part, w_ref in enumerate((wq_ref, wk_ref, wv_ref)):
            u = _dot(hn, w_ref[...])
            for h in range(MOBA_HEADS):
                uh = u[:, h * MOBA_HEAD_DIM:(h + 1) * MOBA_HEAD_DIM]
                if part < 2:
                    ms = jnp.mean(uh * uh, axis=-1, keepdims=True)
                    uh = uh * lax.rsqrt(ms + NORM_EPS) * g_ref[part]
                o_ref[part, 0, h, rows, :] = uh


def _moba_proj(x, gmix, w_in_b, l, g):
    tm = TM_PROJ
    n_s = SEQ // tm
    hd = MOBA_HEADS * MOBA_HEAD_DIM
    w_spec = lambda part: pl.BlockSpec((None, D_MODEL, hd), lambda i: (l, 0, COL_MOBA // hd + part))
    return pl.pallas_call(
        _moba_proj_body,
        grid=(TOKENS // tm,),
        in_specs=[
            pl.BlockSpec((tm, D_MODEL), lambda i: (i, 0)),
            pl.BlockSpec((None, 1, D_MODEL), lambda i: (l, 0, 0)),
            w_spec(0), w_spec(1), w_spec(2),
            pl.BlockSpec((None, 2, 1, MOBA_HEAD_DIM), lambda i: (l, 0, 0, 0)),
        ],
        out_specs=pl.BlockSpec((3, 1, MOBA_HEADS, tm, MOBA_HEAD_DIM),
                               lambda i: (0, i // n_s, 0, i % n_s, 0)),
        out_shape=jax.ShapeDtypeStruct((3, BATCH, MOBA_HEADS, SEQ, MOBA_HEAD_DIM), F32),
        compiler_params=_params("parallel"),
        name="moba_proj",
    )(x, gmix, w_in_b, w_in_b, w_in_b, g)


def _moba_attn_body(slope_ref, q_ref, k_ref, v_ref, o_ref, kb_scr, vb_scr, s_scr):
    blk, nb = MOBA_BLOCK, MOBA_NB
    kb_scr[...] = k_ref[0, 0, 0].astype(BF16)
    vb_scr[...] = v_ref[0, 0, 0].astype(BF16)
    km = jnp.concatenate(
        [jnp.mean(k_ref[0, 0, 0, j * blk:(j + 1) * blk, :], axis=0, keepdims=True) for j in range(nb)], axis=0)
    slope = slope_ref[0][:, 0:1] * LOG2_E
    key_pos = lax.broadcasted_iota(jnp.int32, (1, blk), 1).astype(F32)
    lane = lax.broadcasted_iota(jnp.int32, (blk, nb), 1)

    for n in range(nb):
        q = q_ref[0, 0, 0, n * blk:(n + 1) * blk, :]
        row_bias = None
        if n > MOBA_TOPK:
            cols = []
            gate = jnp.full((blk, nb), NEG_INF, F32)
            for j in range(nb):
                if j < n:
                    gj = jnp.sum(q * km[j:j + 1, :], axis=-1, keepdims=True)
                    gate = jnp.where(lane == j, gj, gate)
                else:
                    gj = jnp.full((blk, 1), NEG_INF, F32)
                cols.append(gj)
            rank = jnp.zeros((blk, nb), jnp.int32)
            for j in range(nb):
                wins = jnp.where(cols[j] > gate, 1, 0)
                wins_or_ties = jnp.where(cols[j] >= gate, 1, 0)
                rank = rank + jnp.where(lane > j, wins_or_ties, wins)
            row_bias = jnp.where((rank < MOBA_TOPK) & (lane < n), 0.0, NEG_INF)

        def bias_fn(j, s, n=n, row_bias=row_bias):
            s = s + slope * (key_pos + float(j * blk))
            if row_bias is not None and j < n:
                s = s + row_bias[:, j:j + 1]
            return s

        qb = (q * (MOBA_HEAD_DIM ** -0.5 * LOG2_E)).astype(BF16)
        o = _attend_tile(n, qb, lambda j: kb_scr[j * blk:(j + 1) * blk, :],
                         lambda j: vb_scr[j * blk:(j + 1) * blk, :], s_scr, bias_fn)
        o_ref[0, n * blk:(n + 1) * blk, :] = o.astype(BF16)


def _moba_attn(qkv, slopes):
    blk = MOBA_BLOCK
    assert blk == TQ_MLA
    spec = lambda which: pl.BlockSpec((1, 1, 1, SEQ, MOBA_HEAD_DIM), lambda b, h: (which, b, h, 0, 0))
    return pl.pallas_call(
        _moba_attn_body,
        grid=(BATCH, MOBA_HEADS),
        in_specs=[pl.BlockSpec((1, 1, 128), lambda b, h: (h, 0, 0)), spec(0), spec(1), spec(2)],
        out_specs=pl.BlockSpec((1, SEQ, MOBA_HEAD_DIM), lambda b, h: (b, 0, h)),
        out_shape=jax.ShapeDtypeStruct((BATCH, SEQ, MOBA_HEADS * MOBA_HEAD_DIM), BF16),
        scratch_shapes=[pltpu.VMEM((SEQ, MOBA_HEAD_DIM), BF16), pltpu.VMEM((SEQ, MOBA_HEAD_DIM), BF16),
                        pltpu.VMEM((blk, N_SCORE_CHUNKS * blk), F32)],
        compiler_params=_params("parallel", "parallel"),
        name="moba_attn",
    )(slopes, qkv, qkv, qkv)


def _merge_body(x_ref, gmix_ref, oa_ref, ob_ref, oc_ref, wga_ref, wgb_ref, wgc_ref, bga_ref, bgb_ref, bgc_ref,
                woa_ref, wob_ref, woc_ref, wout_ref, o_ref, h_scr, acc_scr):
    j = pl.program_id(1)

    @pl.when(j == 0)
    def _():
        h_scr[...] = _rms_bf16(x_ref[...], gmix_ref[...])
        acc_scr[...] = jnp.zeros_like(acc_scr)

    h = h_scr[...]
    merged = None
    for o_i, wg_i, bg_i, wo_i in ((oa_ref, wga_ref, bga_ref, woa_ref), (ob_ref, wgb_ref, bgb_ref, wob_ref),
                                  (oc_ref, wgc_ref, bgc_ref, woc_ref)):
        gate = jax.nn.sigmoid(_dot(h, wg_i[...]) + bg_i[...])
        term = gate * _dot(o_i[...], wo_i[...])
        merged = term if merged is None else merged + term
    acc_scr[...] += _dot(merged.astype(BF16), wout_ref[...])

    @pl.when(j == pl.num_programs(1) - 1)
    def _():
        o_ref[...] = x_ref[...] + acc_scr[...]


def _merge(x, gmix, oa, ob, oc, w_in_b, b_gate, w_oa, w_ob, w_oc, w_out, l):
    tm, tn = TM_MERGE, TN_MERGE
    n_j = D_MODEL // tn
    tok = lambda d: pl.BlockSpec((tm, d), lambda i, j: (i, 0))
    wg = lambda b: pl.BlockSpec((None, D_MODEL, tn), lambda i, j: (l, 0, COL_GATE // tn + b * n_j + j))
    bg = lambda b: pl.BlockSpec((None, 1, tn), lambda i, j: (l, 0, b * n_j + j))
    wo = lambda d: pl.BlockSpec((None, d, tn), lambda i, j: (l, 0, j))
    return pl.pallas_call(
        _merge_body,
        grid=(TOKENS // tm, n_j),
        in_specs=[
            tok(D_MODEL),
            pl.BlockSpec((None, 1, D_MODEL), lambda i, j: (l, 0, 0)),
            tok(MLA_HEADS * MLA_V), tok(CONV_CH), tok(MOBA_HEADS * MOBA_HEAD_DIM),
            wg(0), wg(1), wg(2), bg(0), bg(1), bg(2),
            wo(MLA_HEADS * MLA_V), wo(CONV_CH), wo(MOBA_HEADS * MOBA_HEAD_DIM),
            pl.BlockSpec((None, tn, D_MODEL), lambda i, j: (l, j, 0)),
        ],
        out_specs=tok(D_MODEL),
        out_shape=jax.ShapeDtypeStruct((TOKENS, D_MODEL), F32),
        scratch_shapes=[pltpu.VMEM((tm, D_MODEL), BF16), pltpu.VMEM((tm, D_MODEL), F32)],
        compiler_params=_params("parallel", "arbitrary"),
        name="merge",
    )(x, gmix, oa, ob, oc, w_in_b, w_in_b, w_in_b, b_gate, b_gate, b_gate, w_oa, w_ob, w_oc, w_out)


def _rope_table():
    half = MLA_ROPE // 2
    inv_freq = jnp.exp(-math.log(ROPE_THETA) * jnp.arange(half, dtype=F32) * 2.0 / MLA_ROPE)
    ang = jnp.arange(SEQ, dtype=jnp.int32).astype(F32)[:, None] * inv_freq[None, :]
    cos, sin = jnp.cos(ang), jnp.sin(ang)
    return jnp.concatenate([cos, cos, sin, sin], axis=-1)


def kernel(x, ffn1_norm, ffn1_w_gate, ffn1_w_up, ffn1_w_down, mix_norm, w_in, b_gate, mla_cq_norm, mla_ckv_norm, mla_w_uq, mla_w_ukv, mla_q_norm, mla_k_norm, mla_w_o, conv_w_dw, conv_b_dw, conv_ln_g, conv_ln_b, conv_w_pw, moba_q_norm, moba_k_norm, moba_w_o, w_out, ffn2_norm, ffn2_w_gate, ffn2_w_up, ffn2_w_down):
    cs = _rope_table()
    slopes = jnp.exp2(-8.0 * jnp.arange(1, MOBA_HEADS + 1, dtype=F32) / MOBA_HEADS)
    slopes = jnp.broadcast_to(slopes[:, None, None], (MOBA_HEADS, 1, 128))
    ffn1 = (ffn1_w_gate, ffn1_w_up, ffn1_w_down)
    ffn2 = (ffn2_w_gate, ffn2_w_up, ffn2_w_down)
    w_in_b, w_lat, w_kk = _prep_w_in(w_in)
    wq = mla_w_uq.reshape(DEPTH, MLA_Q_RANK, MLA_HEADS, MLA_QK)
    wq = jnp.concatenate([wq, _rot_cols(wq[..., MLA_NOPE:])], axis=-1).astype(BF16)
    wq = wq.reshape(DEPTH, MLA_Q_RANK, MLA_HEADS * (MLA_NOPE + 2 * MLA_ROPE))
    wkv = mla_w_ukv.astype(BF16)
    w_oa, w_ob, w_oc, w_o = (w.astype(BF16) for w in (mla_w_o, conv_w_pw, moba_w_o, w_out))
    rows = lambda a: a.reshape(DEPTH, 1, a.shape[-1])
    bg, g_ffn1, g_ffn2, gmix = rows(b_gate), rows(ffn1_norm), rows(ffn2_norm), rows(mix_norm)
    gcq, gckv, gq, gk = rows(mla_cq_norm), rows(mla_ckv_norm), rows(mla_q_norm), rows(mla_k_norm)
    b_dw, ln_g, ln_b = rows(conv_b_dw), rows(conv_ln_g), rows(conv_ln_b)
    g_moba = jnp.stack([moba_q_norm, moba_k_norm], axis=1).reshape(DEPTH, 2, 1, MOBA_HEAD_DIM)
    xt = x.reshape(TOKENS, D_MODEL)
    for l in range(DEPTH):
        xt = _ffn(xt, g_ffn1, *ffn1, l)
        q, k, v = _mla_proj(xt, gmix, w_lat, w_kk, l, gcq, gckv, wq, wkv, gq, gk, cs)
        o_mla = _mla_attn(q, k, v).reshape(TOKENS, MLA_HEADS * MLA_V)
        z = _glu_proj(xt, gmix, w_in_b, l)
        o_conv = _conv(z, conv_w_dw, b_dw, ln_g, ln_b, l)
        qkv = _moba_proj(xt, gmix, w_in_b, l, g_moba)
        o_moba = _moba_attn(qkv, slopes).reshape(TOKENS, MOBA_HEADS * MOBA_HEAD_DIM)
        xt = _merge(xt, gmix, o_mla, o_conv, o_moba, w_in_b, bg, w_oa, w_ob, w_oc, w_o, l)
        xt = _ffn(xt, g_ffn2, *ffn2, l)
    return xt.reshape(BATCH, SEQ, D_MODEL)
```
